```python
import math
import numpy as np
import jax
import jax.numpy as jnp
from jax import lax

D_MODEL = 1024
BATCH = 2
SEQ = 8192
DEPTH = 1

NSA_HEADS = 8
NSA_KV_GROUPS = 2
NSA_HPG = NSA_HEADS // NSA_KV_GROUPS
NSA_HEAD_DIM = 64
NSA_WIDTH = NSA_HEADS * NSA_HEAD_DIM
NSA_KV_WIDTH = NSA_KV_GROUPS * NSA_HEAD_DIM
CMP_BLK = 32
CMP_STRIDE = 16
CMP_HID = 256
SEL_BLK = 64
N_SEL = 16
WINDOW = 512
Q_BLK = 128
ROPE_THETA = 10000.0

GDN_HEADS = 4
GDN_HEAD_DIM = 128
GDN_WIDTH = GDN_HEADS * GDN_HEAD_DIM
CONV_K = 4
GDN_CHUNK = 64

MIX_WIDTH = NSA_WIDTH + GDN_WIDTH
IN_SPLITS = (NSA_WIDTH, 6 * NSA_KV_WIDTH, 3 * NSA_HEADS, 3 * GDN_WIDTH, GDN_WIDTH, GDN_HEADS, GDN_HEADS)
IN_COLS = NSA_WIDTH + 6 * NSA_KV_WIDTH + 3 * NSA_HEADS + 4 * GDN_WIDTH + 2 * GDN_HEADS

PEER_HEADS = 8
PEER_TOPK = 16
N_KEYS = 128
N_EXPERTS = N_KEYS * N_KEYS
PEER_DKEY = 256
PEER_TBLK = 128

NEG_INF = -1e30
FORCE = 1e4

kernel_name = 'hybrid_nsa_gdn_peer'


def _rmsnorm(x, g, eps=1e-6):
    xf = x.astype(jnp.float32)
    y = xf * lax.rsqrt(jnp.mean(xf * xf, axis=-1, keepdims=True) + eps)
    return (y * g.astype(jnp.float32)).astype(x.dtype)


def _l2norm(x, eps=1e-6):
    return x * lax.rsqrt(jnp.sum(x * x, axis=-1, keepdims=True) + eps)


def _rope(x, pos):
    d = x.shape[-1]
    half = d // 2
    inv = jnp.exp(-math.log(ROPE_THETA) * 2.0 * jnp.arange(half, dtype=jnp.float32) / d)
    ang = pos.astype(jnp.float32)[:, None] * inv[None, :]
    cos, sin = jnp.cos(ang), jnp.sin(ang)
    xf = x.astype(jnp.float32)
    x1, x2 = xf[..., :half], xf[..., half:]
    return jnp.concatenate([x1 * cos - x2 * sin, x2 * cos + x1 * sin], axis=-1).astype(x.dtype)


def _masked_softmax(s, mask):
    s = jnp.where(mask, s.astype(jnp.float32), NEG_INF)
    p = jax.nn.softmax(s, axis=-1)
    return p * jnp.any(mask, axis=-1, keepdims=True).astype(jnp.float32)


def _cmp_to_sel(p_cmp, n_blk):
    r = SEL_BLK // CMP_STRIDE
    l = CMP_BLK // CMP_STRIDE
    n_cmp = p_cmp.shape[-1]
    pad = [(0, 0)] * (p_cmp.ndim - 1) + [(l - 1, r * n_blk - n_cmp)]
    padded = jnp.pad(p_cmp, pad)
    out = jnp.zeros(p_cmp.shape[:-1] + (n_blk,), jnp.float32)
    for o in range(-(l - 1), r):
        w = float(min(o + l, r) - max(o, 0))
        start = o + l - 1
        out = out + w * padded[..., start:start + r * n_blk:r]
    return out


def _nsa(q_in, kc_in, vc_in, ks_in, vs_in, kw_in, vw_in, gate_in, q_gain, k_gain, cmp_pos, cmp_w1, cmp_w2):
    B, S, _ = q_in.shape
    G, HPG, DH = NSA_KV_GROUPS, NSA_HPG, NSA_HEAD_DIM
    pos = jnp.arange(S)
    q = q_in.reshape(B, S, G, HPG, DH).transpose(0, 2, 3, 1, 4)
    q = _rope(_rmsnorm(q, q_gain), pos)

    def heads(a):
        return a.reshape(B, S, G, DH).transpose(0, 2, 1, 3)

    l = CMP_BLK // CMP_STRIDE
    n_cmp = S // CMP_STRIDE - l + 1

    def compress(a, j):
        a16 = a.reshape(B, G, S // CMP_STRIDE, CMP_STRIDE, DH)
        blocks = jnp.concatenate([a16[:, :, i:i + n_cmp] for i in range(l)], axis=3)
        blocks = (blocks + cmp_pos[j]).reshape(B, G, n_cmp, CMP_BLK * DH)
        return jax.nn.gelu(blocks @ cmp_w1[j]) @ cmp_w2[j]

    cmp_end = jnp.arange(n_cmp) * CMP_STRIDE + CMP_BLK - 1
    kc = _rope(_rmsnorm(compress(heads(kc_in), 0), k_gain[0]), cmp_end)
    vc = compress(heads(vc_in), 1)
    n_blk = S // SEL_BLK
    ks = _rope(_rmsnorm(heads(ks_in), k_gain[1]), pos)
    ks_blk = ks.reshape(B, G, n_blk, SEL_BLK, DH)
    vs_blk = heads(vs_in).reshape(B, G, n_blk, SEL_BLK, DH)
    kw = _rope(_rmsnorm(heads(kw_in), k_gain[2]), pos)
    kw_pad = jnp.pad(kw, ((0, 0), (0, 0), (WINDOW, 0), (0, 0)))
    vw_pad = jnp.pad(heads(vw_in), ((0, 0), (0, 0), (WINDOW, 0), (0, 0)))

    n_sel = min(N_SEL, n_blk)
    scale = DH ** -0.5
    bi = jnp.arange(B)[:, None, None, None]
    gi = jnp.arange(G)[None, :, None, None]
    blk_ids = jnp.arange(n_blk)
    sel_off = jnp.arange(SEL_BLK)
    win_off = jnp.arange(WINDOW + Q_BLK) - WINDOW

    def block(args):
        qb, qq = args
        t0 = qb * Q_BLK
        t = t0 + jnp.arange(Q_BLK)
        s_c = jnp.einsum('bghtd,bgnd->bghtn', qq, kc) * scale
        p_c = _masked_softmax(s_c, cmp_end[None, :] <= t[:, None])
        o_c = jnp.einsum('bghtn,bgnd->bghtd', p_c.astype(vc.dtype), vc)
        imp = _cmp_to_sel(p_c, n_blk).sum(axis=2)
        cur = t // SEL_BLK
        forced = (blk_ids[None, :] == 0) | (blk_ids[None, :] == cur[:, None]) | (blk_ids[None, :] == cur[:, None] - 1)
        causal_blk = blk_ids[None, :] * SEL_BLK <= t[:, None]
        score = jnp.where(causal_blk, jnp.where(forced, FORCE, imp), NEG_INF)
        _, idx = lax.top_k(score, n_sel)
        k_g = ks_blk[bi, gi, idx].reshape(B, G, Q_BLK, n_sel * SEL_BLK, DH)
        v_g = vs_blk[bi, gi, idx].reshape(B, G, Q_BLK, n_sel * SEL_BLK, DH)
        pos_g = (idx[..., None] * SEL_BLK + sel_off).reshape(B, G, Q_BLK, n_sel * SEL_BLK)
        s_s = jnp.einsum('bghtd,bgtkd->bghtk', qq, k_g) * scale
        p_s = _masked_softmax(s_s, (pos_g <= t[:, None])[:, :, None])
        o_s = jnp.einsum('bghtk,bgtkd->bghtd', p_s.astype(v_g.dtype), v_g)
        k_w = lax.dynamic_slice_in_dim(kw_pad, t0, WINDOW + Q_BLK, axis=2)
        v_w = lax.dynamic_slice_in_dim(vw_pad, t0, WINDOW + Q_BLK, axis=2)
        pos_w = t0 + win_off
        rel = t[:, None] - pos_w[None, :]
        mask_w = (pos_w[None, :] >= 0) & (rel >= 0) & (rel < WINDOW)
        s_w = jnp.einsum('bghtd,bgkd->bghtk', qq, k_w) * scale
        p_w = _masked_softmax(s_w, mask_w)
        o_w = jnp.einsum('bghtk,bgkd->bghtd', p_w.astype(v_w.dtype), v_w)
        return o_c, o_s, o_w

    n_qb = S // Q_BLK
    q_blocks = q.reshape(B, G, HPG, n_qb, Q_BLK, DH).transpose(3, 0, 1, 2, 4, 5)
    o_c, o_s, o_w = lax.map(block, (jnp.arange(n_qb), q_blocks))
    o = jnp.stack([o_c, o_s, o_w], axis=-1)
    o = o.transpose(1, 0, 4, 2, 3, 5, 6).reshape(B, S, NSA_HEADS, DH, 3)
    gate = jax.nn.sigmoid(gate_in.astype(jnp.float32)).reshape(B, S, NSA_HEADS, 1, 3)
    return (o.astype(jnp.float32) * gate).sum(-1).reshape(B, S, NSA_WIDTH).astype(q_in.dtype)


def _gated_delta_chunked(q, k, v, g, beta):
    B, H, S, dk = q.shape
    dv = v.shape[-1]
    C = GDN_CHUNK
    N = S // C
    q = q.reshape(B, H, N, C, dk)
    k = k.reshape(B, H, N, C, dk)
    v = v.reshape(B, H, N, C, dv)
    beta = beta.reshape(B, H, N, C)
    g = jnp.cumsum(g.reshape(B, H, N, C), axis=-1)
    incl = jnp.tril(jnp.ones((C, C), bool))
    strict = jnp.tril(jnp.ones((C, C), bool), -1)
    diff = g[..., :, None] - g[..., None, :]
    decay = jnp.where(incl, jnp.exp(jnp.where(incl, diff, 0.0)), 0.0)
    kb = k * beta[..., None]
    a_mat = jnp.where(strict, jnp.einsum('bhnid,bhnjd->bhnij', kb, k) * decay, 0.0)
    m = a_mat + jnp.eye(C, dtype=jnp.float32)
    u = lax.linalg.triangular_solve(m, v * beta[..., None], left_side=True, lower=True)
    w = lax.linalg.triangular_solve(m, kb * jnp.exp(g)[..., None], left_side=True, lower=True)
    intra = jnp.where(incl, jnp.einsum('bhnid,bhnjd->bhnij', q, k) * decay, 0.0)
    q_dec = q * jnp.exp(g)[..., None]
    k_dec = k * jnp.exp(g[..., -1:] - g)[..., None]
    g_last = jnp.exp(g[..., -1])

    def step(state, xs):
        qd, kd, u_c, w_c, a_c, gl = xs
        v_new = u_c - jnp.einsum('bhck,bhkv->bhcv', w_c, state)
        o = jnp.einsum('bhck,bhkv->bhcv', qd, state) + jnp.einsum('bhcj,bhjv->bhcv', a_c, v_new)
        state = state * gl[..., None, None] + jnp.einsum('bhck,bhcv->bhkv', kd, v_new)
        return state, o

    xs = tuple(jnp.moveaxis(a, 2, 0) for a in (q_dec, k_dec, u, w, intra, g_last))
    _, o = lax.scan(step, jnp.zeros((B, H, dk, dv), jnp.float32), xs)
    return jnp.moveaxis(o, 0, 2).reshape(B, H, S, dv)


def _gdn(qkv, z, a, b, conv_w, a_log, dt_bias, out_gain):
    B, S, _ = qkv.shape
    H, DH = GDN_HEADS, GDN_HEAD_DIM
    xp = jnp.pad(qkv, ((0, 0), (CONV_K - 1, 0), (0, 0)))
    conv = xp[:, 0:S] * conv_w[0]
    for j in range(1, CONV_K):
        conv = conv + xp[:, j:j + S] * conv_w[j]
    conv = jax.nn.silu(conv)
    q, k, v = jnp.split(conv, 3, axis=-1)

    def heads(t):
        return t.reshape(B, S, H, DH).transpose(0, 2, 1, 3).astype(jnp.float32)

    q = _l2norm(heads(q)) * (DH ** -0.5)
    k = _l2norm(heads(k))
    v = heads(v)
    beta = jax.nn.sigmoid(b.astype(jnp.float32)).transpose(0, 2, 1)
    g = (-jnp.exp(a_log.astype(jnp.float32)) * jax.nn.softplus(a.astype(jnp.float32) + dt_bias.astype(jnp.float32))).transpose(0, 2, 1)
    o = _gated_delta_chunked(q, k, v, g, beta).transpose(0, 2, 1, 3)
    o = _rmsnorm(o, out_gain) * jax.nn.silu(z.reshape(B, S, H, DH).astype(jnp.float32))
    return o.reshape(B, S, GDN_WIDTH).astype(qkv.dtype)


def _peer(h, w_query, subkeys, u_tab, v_tab):
    B, S, D = h.shape
    T = B * S
    hf = h.reshape(T, D)
    qry = (hf @ w_query).reshape(T, PEER_HEADS, 2, PEER_DKEY // 2).astype(jnp.float32)
    s1 = jnp.einsum('thd,nd->thn', qry[:, :, 0], subkeys[0].astype(jnp.float32))
    s2 = jnp.einsum('thd,nd->thn', qry[:, :, 1], subkeys[1].astype(jnp.float32))
    v1, i1 = lax.top_k(s1, PEER_TOPK)
    v2, i2 = lax.top_k(s2, PEER_TOPK)
    cand = (v1[..., :, None] + v2[..., None, :]).reshape(T, PEER_HEADS, PEER_TOPK * PEER_TOPK)
    cidx = (i1[..., :, None] * N_KEYS + i2[..., None, :]).reshape(T, PEER_HEADS, PEER_TOPK * PEER_TOPK)
    top_s, top_pos = lax.top_k(cand, PEER_TOPK)
    eidx = jnp.take_along_axis(cidx, top_pos, axis=-1)
    gate = jax.nn.softmax(top_s, axis=-1)
    nb = T // PEER_TBLK

    def block(args):
        xb, eb, gb = args
        ue = u_tab[eb]
        act = jax.nn.gelu(jnp.einsum('td,thkd->thk', xb, ue).astype(jnp.float32))
        coef = (gb * act).astype(v_tab.dtype)
        return jnp.einsum('thk,thkd->td', coef, v_tab[eb])

    out = lax.map(block, (hf.reshape(nb, PEER_TBLK, D),
                          eidx.reshape(nb, PEER_TBLK, PEER_HEADS, PEER_TOPK),
                          gate.reshape(nb, PEER_TBLK, PEER_HEADS, PEER_TOPK)))
    return out.reshape(B, S, D).astype(h.dtype)


def setup_inputs(seed: int = 0) -> dict:
    key = jax.random.key(seed)
    ks = jax.random.split(key, 20)
    f32 = jnp.float32
    nrm = lambda k, shape, s: jax.random.normal(k, shape, f32) * s
    dt = jnp.exp(jax.random.uniform(ks[10], (DEPTH, GDN_HEADS), f32, math.log(1e-3), math.log(1e-1)))
    return {
        'x': nrm(ks[0], (BATCH, SEQ, D_MODEL), 1.0),
        'attn_norm': 1.0 + nrm(ks[1], (DEPTH, D_MODEL), 0.02),
        'w_in': nrm(ks[2], (DEPTH, D_MODEL, IN_COLS), D_MODEL ** -0.5),
        'nsa_q_gain': 1.0 + nrm(ks[3], (DEPTH, NSA_HEAD_DIM), 0.02),
        'nsa_k_gain': 1.0 + nrm(ks[4], (DEPTH, 3, NSA_HEAD_DIM), 0.02),
        'cmp_pos': nrm(ks[5], (DEPTH, 2, CMP_BLK, NSA_HEAD_DIM), 0.1),
        'cmp_w1': nrm(ks[6], (DEPTH, 2, CMP_BLK * NSA_HEAD_DIM, CMP_HID), (CMP_BLK * NSA_HEAD_DIM) ** -0.5),
        'cmp_w2': nrm(ks[7], (DEPTH, 2, CMP_HID, NSA_HEAD_DIM), CMP_HID ** -0.5),
        'gdn_conv': nrm(ks[8], (DEPTH, CONV_K, 3 * GDN_WIDTH), CONV_K ** -0.5),
        'gdn_a_log': jnp.log(jax.random.uniform(ks[9], (DEPTH, GDN_HEADS), f32, 1.0, 16.0)),
        'gdn_dt_bias': dt + jnp.log(-jnp.expm1(-dt)),
        'gdn_out_gain': 1.0 + nrm(ks[11], (DEPTH, GDN_HEAD_DIM), 0.02),
        'w_out': nrm(ks[12], (DEPTH, MIX_WIDTH, D_MODEL), MIX_WIDTH ** -0.5),
        'ffn_norm': 1.0 + nrm(ks[13], (DEPTH, D_MODEL), 0.02),
        'peer_w_query': nrm(ks[14], (DEPTH, D_MODEL, PEER_HEADS * PEER_DKEY), D_MODEL ** -0.5),
        'peer_subkeys': nrm(ks[15], (DEPTH, 2, N_KEYS, PEER_DKEY // 2), (PEER_DKEY // 2) ** -0.5),
        'peer_u': nrm(ks[16], (DEPTH, N_EXPERTS, D_MODEL), D_MODEL ** -0.5),
        'peer_v': nrm(ks[17], (DEPTH, N_EXPERTS, D_MODEL), 0.5),
    }


def reference(x, attn_norm, w_in, nsa_q_gain, nsa_k_gain, cmp_pos, cmp_w1, cmp_w2, gdn_conv, gdn_a_log,
              gdn_dt_bias, gdn_out_gain, w_out, ffn_norm, peer_w_query, peer_subkeys, peer_u, peer_v):
    offsets = np.cumsum(np.array(IN_SPLITS))[:-1].tolist()
    for l in range(DEPTH):
        h = _rmsnorm(x, attn_norm[l])
        proj = h @ w_in[l]
        q_a, kv_a, gate_a, qkv_b, z_b, a_b, b_b = jnp.split(proj, offsets, axis=-1)
        kc, vc, ks_, vs_, kw, vw = jnp.split(kv_a, 6, axis=-1)
        o_a = _nsa(q_a, kc, vc, ks_, vs_, kw, vw, gate_a, nsa_q_gain[l], nsa_k_gain[l],
                   cmp_pos[l], cmp_w1[l], cmp_w2[l])
        o_b = _gdn(qkv_b, z_b, a_b, b_b, gdn_conv[l], gdn_a_log[l], gdn_dt_bias[l], gdn_out_gain[l])
        x = x + jnp.concatenate([o_a, o_b], axis=-1) @ w_out[l]
        x = x + _peer(_rmsnorm(x, ffn_norm[l]), peer_w_query[l], peer_subkeys[l], peer_u[l], peer_v[l])
    return x
```

```python
import functools
import math

import jax
import jax.numpy as jnp
import numpy as np
from jax import lax
from jax.experimental import pallas as pl
from jax.experimental.pallas import tpu as pltpu

F32 = jnp.float32
BF16 = jnp.bfloat16
HI = lax.Precision.HIGHEST

D_MODEL = 1024
NSA_HEADS = 8
NSA_GROUPS = 2
NSA_HPG = NSA_HEADS // NSA_GROUPS
NSA_DH = 64
CMP_BLK = 32
CMP_STRIDE = 16
CMP_HID = 256
SEL_BLK = 64
N_SEL = 16
WINDOW = 512
Q_BLK = 128
ROPE_THETA = 10000.0
GDN_HEADS = 4
GDN_DH = 128
CONV_K = 4
GDN_CHUNK = 64
PEER_HEADS = 8
PEER_TOPK = 16
N_KEYS = 128
PEER_DKEY = 256
NEG = -1e30
MASK_OFF = 1e30
FORCE = 1e4
EPS = 1e-6

KEY_TILE = 1024
LANES = 128
VMEM_LIMIT = 56 * 1024 * 1024


def _cparams(sem):
    return pltpu.CompilerParams(dimension_semantics=sem, vmem_limit_bytes=VMEM_LIMIT)


def _dot(a, b, precision=None):
    return jnp.dot(a, b, preferred_element_type=F32, precision=precision)


def _dot_nt(a, b, precision=None):
    return lax.dot_general(a, b, (((1,), (1,)), ((), ())), preferred_element_type=F32, precision=precision)


def _dot_tn(a, b):
    return lax.dot_general(a, b, (((0,), (0,)), ((), ())), preferred_element_type=F32)


def _iota(shape, axis):
    return lax.broadcasted_iota(jnp.int32, shape, axis)


def _real_const(c):
    hi = np.float32(c)
    lo = np.float32(c - float(hi))
    return jnp.asarray(hi) + jnp.asarray(lo)


def _gelu_tanh(x, c0, c1):
    return x * (0.5 * (1.0 + jnp.tanh(c0 * (x + c1 * (x * x * x)))))


def _pair_norm_rope(x, seg, gain, cos, sinsg):
    ms = _dot(x * x, seg, HI)
    y = x * lax.rsqrt(ms + EPS) * gain
    lane = _iota(y.shape, 1) % NSA_DH
    swapped = jnp.where(lane < NSA_DH // 2, pltpu.roll(y, LANES - NSA_DH // 2, 1), pltpu.roll(y, NSA_DH // 2, 1))
    return y * cos + swapped * sinsg


def _proj_body(x_ref, g_ref, w_ref, seg_ref, cos_ref, sin_ref, qg_ref, kg_ref,
               q_ref, kv_ref, qkvb_ref, zb_ref, sm_ref):
    x = x_ref[...]
    r = lax.rsqrt(jnp.mean(x * x, axis=-1, keepdims=True) + EPS)
    h = (x * r * g_ref[...]).astype(BF16)
    seg = seg_ref[...]
    cos = cos_ref[...]
    sin = sin_ref[...]
    scale = NSA_DH ** -0.5
    for j in range(4):
        p = _dot(h, w_ref[:, j * LANES:(j + 1) * LANES])
        q_ref[:, j * LANES:(j + 1) * LANES] = (_pair_norm_rope(p, seg, qg_ref[...], cos, sin) * scale).astype(BF16)
    for j in range(6):
        p = _dot(h, w_ref[:, 512 + j * LANES:512 + (j + 1) * LANES])
        if j == 2:
            p = _pair_norm_rope(p, seg, kg_ref[1:2, :], cos, sin)
        elif j == 4:
            p = _pair_norm_rope(p, seg, kg_ref[2:3, :], cos, sin)
        kv_ref[:, j * LANES:(j + 1) * LANES] = p
    qkvb_ref[...] = _dot(h, w_ref[:, 1280:2816])
    zb_ref[...] = _dot(h, w_ref[:, 2816:3328])
    sm_ref[...] = _dot(h, w_ref[:, 3328:3456])


def _proj(x2d, g, w, seg, cos, sin, qg, kg, seq, tm=256):
    t = x2d.shape[0]
    nps = seq // tm
    full = lambda shape: pl.BlockSpec(shape, lambda i: (0,) * len(shape))
    row = lambda n: pl.BlockSpec((tm, n), lambda i: (i, 0))
    return pl.pallas_call(
        _proj_body,
        grid=(t // tm,),
        in_specs=[row(D_MODEL), full((1, D_MODEL)), full(w.shape), full((LANES, LANES)),
                  pl.BlockSpec((tm, LANES), lambda i: (i % nps, 0)),
                  pl.BlockSpec((tm, LANES), lambda i: (i % nps, 0)),
                  full((1, LANES)), full((3, LANES))],
        out_specs=[row(512), row(768), row(1536), row(512), row(LANES)],
        out_shape=[jax.ShapeDtypeStruct((t, 512), BF16), jax.ShapeDtypeStruct((t, 768), F32),
                   jax.ShapeDtypeStruct((t, 1536), F32), jax.ShapeDtypeStruct((t, 512), F32),
                   jax.ShapeDtypeStruct((t, LANES), F32)],
        compiler_params=_cparams(("parallel",)),
        name="proj",
    )(x2d, g, w, seg, cos, sin, qg, kg)


def _compress_body(ak_ref, av_ref, w1_ref, w2_ref, w2p_ref, pos_ref, gain_ref, gainp_ref, cos_ref, sin_ref, gc_ref,
                   kc_ref, vc_ref):
    n = ak_ref.shape[0]
    half = ak_ref.shape[1]

    def hidden(a_ref, j):
        a = a_ref[...].astype(BF16)
        ha = _dot(a, w1_ref[j, 0:half, :])
        hb = _dot(a, w1_ref[j, half:2 * half, :])
        pb = _dot(pos_ref[j], w1_ref[j])
        hid = ha + pltpu.roll(hb, n - 1, 0) + pb[0:1, :]
        return _gelu_tanh(hid, gc_ref[0:1, 0:1], gc_ref[0:1, 1:2]).astype(BF16)

    hk = hidden(ak_ref, 0)
    ck = _dot(hk, w2_ref[0])
    ckp = _dot(hk, w2p_ref[...])
    r = lax.rsqrt(jnp.mean(ck * ck, axis=-1, keepdims=True) + EPS)
    kc = ck * r * gain_ref[...] * cos_ref[...] + ckp * r * gainp_ref[...] * sin_ref[...]
    kc_ref[...] = kc.astype(BF16)
    hv = hidden(av_ref, 1)
    vc_ref[...] = _dot(hv, w2_ref[1]).astype(BF16)


def _compress(a16k, a16v, w1, w2, w2p, pos8, gain, gainp, cos, sin, gelu_consts):
    b, g, n, half = a16k.shape
    full = lambda shape: pl.BlockSpec(shape, lambda i, j: (0,) * len(shape))
    per = lambda last: pl.BlockSpec((None, None, n, last), lambda i, j: (i, j, 0, 0))
    return pl.pallas_call(
        _compress_body,
        grid=(b, g),
        in_specs=[per(half), per(half), full(w1.shape), full(w2.shape), full(w2p.shape), full(pos8.shape),
                  full(gain.shape), full(gainp.shape), full(cos.shape), full(sin.shape), full(gelu_consts.shape)],
        out_specs=[per(NSA_DH), per(NSA_DH)],
        out_shape=[jax.ShapeDtypeStruct((b, g, n, NSA_DH), BF16)] * 2,
        compiler_params=_cparams(("parallel", "parallel")),
        name="compress",
    )(a16k, a16v, w1, w2, w2p, pos8, gain, gainp, cos, sin, gelu_consts)


def _col_softmax(s):
    e = jnp.exp(s - jnp.max(s, axis=0, keepdims=True))
    return e, jnp.sum(e, axis=0, keepdims=True)


def _nsa_body(q_ref, kc_ref, vc_ref, ke_ref, vs_ref, kw_ref, vw_ref, gate_ref, wimp_ref, o_ref, *, n_sel):
    n_cmp = kc_ref.shape[0]
    n_blk = wimp_ref.shape[0]
    nq = NSA_HPG * Q_BLK
    t0 = pl.program_id(2) * Q_BLK
    t_row = t0 + _iota((1, Q_BLK), 1)
    t_row4 = t0 + _iota((1, nq), 1) % Q_BLK
    q_t = q_ref[...]
    gates = jax.nn.sigmoid(gate_ref[...])
    lanes = lambda h: slice(h * Q_BLK, (h + 1) * Q_BLK)

    s_c = _dot(kc_ref[...], q_t)
    cmp_end = _iota((n_cmp, 1), 0) * CMP_STRIDE + (CMP_BLK - 1)
    bias_c = jnp.where(cmp_end <= t_row, 0.0, NEG)
    any_c = (t_row >= CMP_BLK - 1).astype(F32)
    vc_t = vc_ref[...]
    o_c = []
    p_sum = jnp.zeros((n_cmp, Q_BLK), F32)
    for h in range(NSA_HPG):
        e, l = _col_softmax(s_c[:, lanes(h)] + bias_c)
        p = e * (any_c / l)
        p_sum = p_sum + p
        o_c.append(_dot(vc_t, p.astype(BF16)))
    imp = _dot(wimp_ref[...], p_sum, HI)

    blk = _iota((n_blk, Q_BLK), 0)
    cur = t_row // SEL_BLK
    forced = (blk == 0) | (blk == cur) | (blk == cur - 1)
    causal = blk * SEL_BLK <= t_row
    score = jnp.where(causal, jnp.where(forced, FORCE, imp), NEG)
    sel = jnp.zeros((n_blk, Q_BLK), F32)
    for _ in range(n_sel):
        m = jnp.max(score, axis=0, keepdims=True)
        idx = jnp.min(jnp.where(score == m, blk, n_blk), axis=0, keepdims=True)
        hit = blk == idx
        sel = jnp.where(hit, 1.0, sel)
        score = jnp.where(hit, -jnp.inf, score)
    nsel = jnp.where(causal, 1.0 - sel, 1.0).astype(BF16)
    bmat = jnp.concatenate([jnp.concatenate([nsel] * NSA_HPG, axis=1), q_t], axis=0)

    def sel_tile(kt, carry, diagonal):
        m, l, acc = carry
        start = pl.multiple_of(kt * KEY_TILE, KEY_TILE)
        s = _dot(ke_ref[pl.ds(start, KEY_TILE), :], bmat)
        if diagonal:
            s = jnp.where(start + _iota((KEY_TILE, 1), 0) <= t_row4, s, NEG)
        m_new = jnp.maximum(m, jnp.max(s, axis=0, keepdims=True))
        alpha = jnp.exp(m - m_new)
        p = jnp.exp(s - m_new)
        l = alpha * l + jnp.sum(p, axis=0, keepdims=True)
        acc = alpha * acc + _dot(vs_ref[kt], p.astype(BF16))
        return m_new, l, acc

    init = (jnp.full((1, nq), NEG, F32), jnp.zeros((1, nq), F32), jnp.zeros((NSA_DH, nq), F32))
    n_full = t0 // KEY_TILE
    carry = lax.fori_loop(0, n_full, functools.partial(sel_tile, diagonal=False), init)
    _, l_s, acc_s = sel_tile(n_full, carry, True)
    o_s = acc_s / l_s

    wspan = WINDOW + Q_BLK
    wstart = pl.multiple_of(jnp.maximum(t0 - WINDOW, 0), Q_BLK)
    s_w = _dot(kw_ref[pl.ds(wstart, wspan), :], q_t)
    w0 = wstart // Q_BLK
    vwin_t = jnp.concatenate([vw_ref[w0 + i] for i in range(wspan // Q_BLK)], axis=1)
    rel = t_row - (wstart + _iota((wspan, 1), 0))
    bias_w = jnp.where((rel >= 0) & (rel < WINDOW), 0.0, NEG)

    for h in range(NSA_HPG):
        e, l = _col_softmax(s_w[:, lanes(h)] + bias_w)
        o_w = _dot(vwin_t, e.astype(BF16)) / l
        g = gates[:, lanes(h)]
        o = o_c[h] * g[0:1] + o_s[:, lanes(h)] * g[1:2] + o_w * g[2:3]
        o_ref[:, lanes(h)] = o.astype(o_ref.dtype)


def _nsa(q_t, kc, vc_t, ke, vs_t, kw, vw_t, gate, wimp):
    b, _, n_qb, _, nq = q_t.shape
    s = n_qb * Q_BLK
    n_cmp = kc.shape[2]
    n_blk = s // SEL_BLK
    per_g = lambda a: pl.BlockSpec((None, None) + a.shape[2:], lambda i, j, k: (i, j) + (0,) * (a.ndim - 2))
    per_q = lambda rows: pl.BlockSpec((None, None, None, rows, nq), lambda i, j, k: (i, j, k, 0, 0))
    return pl.pallas_call(
        functools.partial(_nsa_body, n_sel=min(N_SEL, n_blk)),
        grid=(b, NSA_GROUPS, n_qb),
        in_specs=[per_q(NSA_DH), per_g(kc), per_g(vc_t), per_g(ke), per_g(vs_t), per_g(kw), per_g(vw_t), per_q(3),
                  pl.BlockSpec(wimp.shape, lambda i, j, k: (0, 0))],
        out_specs=per_q(NSA_DH),
        out_shape=jax.ShapeDtypeStruct((b, NSA_GROUPS, n_qb, NSA_DH, nq), BF16),
        compiler_params=_cparams(("parallel", "parallel", "arbitrary")),
        name="nsa",
    )(q_t, kc, vc_t, ke, vs_t, kw, vw_t, gate, wimp)


def _gdn_prep_body(x_ref, prev_ref, w_ref, qs_ref, q_ref, k_ref, v_ref, pad_ref):
    tm = x_ref.shape[0]
    width = GDN_HEADS * GDN_DH
    first = pl.program_id(1) == 0
    pad_ref[0:8, :] = jnp.where(first, 0.0, prev_ref[...])
    pad_ref[8:8 + tm, :] = x_ref[...]
    conv = pad_ref[8:8 + tm, :] * w_ref[CONV_K - 1:CONV_K, :]
    for j in range(CONV_K - 1):
        sh = CONV_K - 1 - j
        conv = conv + pad_ref[8 - sh:8 - sh + tm, :] * w_ref[j:j + 1, :]
    conv = conv * jax.nn.sigmoid(conv)
    for part, ref in enumerate((q_ref, k_ref, v_ref)):
        for h in range(GDN_HEADS):
            lo = part * width + h * GDN_DH
            c = conv[:, lo:lo + GDN_DH]
            if part < 2:
                c = c * lax.rsqrt(jnp.sum(c * c, axis=-1, keepdims=True) + qs_ref[1:2, 0:1])
            if part == 0:
                c = c * qs_ref[0:1, :]
            ref[:, h * GDN_DH:(h + 1) * GDN_DH] = c


def _gdn_prep(qkv, conv_w, q_scale, tm=256):
    b, s, w3 = qkv.shape
    width = w3 // 3
    out = pl.BlockSpec((None, tm, width), lambda i, j: (i, j, 0))
    return pl.pallas_call(
        _gdn_prep_body,
        grid=(b, s // tm),
        in_specs=[pl.BlockSpec((None, tm, w3), lambda i, j: (i, j, 0)),
                  pl.BlockSpec((None, 8, w3), lambda i, j: (i, jnp.maximum(j * (tm // 8) - 1, 0), 0)),
                  pl.BlockSpec(conv_w.shape, lambda i, j: (0, 0)),
                  pl.BlockSpec(q_scale.shape, lambda i, j: (0, 0))],
        out_specs=[out, out, out],
        out_shape=[jax.ShapeDtypeStruct((b, s, width), F32)] * 3,
        scratch_shapes=[pltpu.VMEM((tm + 8, w3), F32)],
        compiler_params=_cparams(("parallel", "parallel")),
        name="gdn_prep",
    )(qkv, qkv, conv_w, q_scale)


def _gdn_local_body(q_ref, k_ref, kt_ref, v_ref, ac_ref, bc_ref, ar_ref, alog_c_ref, dtb_c_ref, alog_r_ref, dtb_r_ref,
                    u_ref, w_ref, intra_ref, qd_ref, kdt_ref, gl_ref, *, nc):
    c = GDN_CHUNK
    row = _iota((c, c), 0)
    col = _iota((c, c), 1)
    incl = row >= col
    strict = row > col
    tri = incl.astype(F32)
    eye = (row == col).astype(F32)
    xs, ts, rhs_u, rhs_w, where_to = [], [], [], [], []
    for ci in range(nc):
        rs = slice(ci * c, (ci + 1) * c)
        g_c = -jnp.exp(alog_c_ref[...]) * jax.nn.softplus(ac_ref[rs, :] + dtb_c_ref[...])
        g_r = -jnp.exp(alog_r_ref[...]) * jax.nn.softplus(ar_ref[:, rs] + dtb_r_ref[...])
        gc_all = _dot(tri, g_c, HI)
        gr_all = _dot_nt(g_r, tri, HI)
        beta_all = jax.nn.sigmoid(bc_ref[rs, :])
        gl_ref[ci] = jnp.exp(gc_all[c - 1:c, :])
        for h in range(GDN_HEADS):
            hs = slice(h * GDN_DH, (h + 1) * GDN_DH)
            gc = gc_all[:, h:h + 1]
            gr = gr_all[h:h + 1, :]
            g_last = gc_all[c - 1:c, h:h + 1]
            beta = beta_all[:, h:h + 1]
            q = q_ref[rs, hs]
            kt = kt_ref[h, ci]
            kb = k_ref[rs, hs] * beta
            decay = jnp.where(incl, jnp.exp(jnp.where(incl, gc - gr, 0.0)), 0.0)
            a = jnp.where(strict, _dot(kb, kt) * decay, 0.0)
            eg = jnp.exp(gc)
            xs.append(-a)
            ts.append(eye - a)
            rhs_u.append(v_ref[rs, hs] * beta)
            rhs_w.append(kb * eg)
            where_to.append((rs, hs))
            intra_ref[h, ci] = jnp.where(incl, _dot(q, kt) * decay, 0.0)
            qd_ref[rs, hs] = q * eg
            kdt_ref[h, ci] = kt * jnp.exp(g_last - gr)
    for _ in range(int(math.log2(c)) - 1):
        xs = [_dot(x, x, HI) for x in xs]
        ts = [t + _dot(t, x, HI) for t, x in zip(ts, xs)]
    for t, ru, rw, (rs, hs) in zip(ts, rhs_u, rhs_w, where_to):
        u_ref[rs, hs] = _dot(t, ru, HI)
        w_ref[rs, hs] = _dot(t, rw, HI)


def _gdn_local(q, k, kt5, v, a_col, b_col, a_row, alog, dtb, nc=2):
    b, s, width = q.shape
    n = s // GDN_CHUNK
    hh = GDN_HEADS
    rows = nc * GDN_CHUNK
    tok = pl.BlockSpec((None, rows, width), lambda i, j: (i, j, 0))
    per5 = lambda d0, d1: pl.BlockSpec((None, hh, nc, d0, d1), lambda i, j: (i, 0, j, 0, 0))
    small = lambda shape: pl.BlockSpec(shape, lambda i, j: (0,) * len(shape))
    return pl.pallas_call(
        functools.partial(_gdn_local_body, nc=nc),
        grid=(b, n // nc),
        in_specs=[tok, tok, per5(GDN_DH, GDN_CHUNK), tok,
                  pl.BlockSpec((None, rows, hh), lambda i, j: (i, j, 0)),
                  pl.BlockSpec((None, rows, hh), lambda i, j: (i, j, 0)),
                  pl.BlockSpec((None, hh, rows), lambda i, j: (i, 0, j)),
                  small((1, hh)), small((1, hh)), small((hh, 1)), small((hh, 1))],
        out_specs=[tok, tok, per5(GDN_CHUNK, GDN_CHUNK), tok, per5(GDN_DH, GDN_CHUNK),
                   pl.BlockSpec((None, nc, 1, hh), lambda i, j: (i, j, 0, 0))],
        out_shape=[jax.ShapeDtypeStruct((b, s, width), F32), jax.ShapeDtypeStruct((b, s, width), F32),
                   jax.ShapeDtypeStruct((b, hh, n, GDN_CHUNK, GDN_CHUNK), F32),
                   jax.ShapeDtypeStruct((b, s, width), F32),
                   jax.ShapeDtypeStruct((b, hh, n, GDN_DH, GDN_CHUNK), F32),
                   jax.ShapeDtypeStruct((b, n, 1, hh), F32)],
        compiler_params=_cparams(("parallel", "parallel")),
        name="gdn_local",
    )(q, k, kt5, v, a_col, b_col, a_row, alog.reshape(1, hh), dtb.reshape(1, hh), alog.reshape(hh, 1),
      dtb.reshape(hh, 1))


def _gdn_scan_body(u_ref, w_ref, intra_ref, qd_ref, kdt_ref, gl_ref, z_ref, gain_ref, o_ref, state_ref, *, nc):
    c = GDN_CHUNK

    @pl.when(pl.program_id(1) == 0)
    def _():
        state_ref[...] = jnp.zeros_like(state_ref)

    heads = range(GDN_HEADS)
    hs = [slice(h * GDN_DH, (h + 1) * GDN_DH) for h in heads]
    for ci in range(nc):
        rs = slice(ci * c, (ci + 1) * c)
        gl = gl_ref[ci]
        st = [state_ref[h] for h in heads]
        v_new = [u_ref[rs, hs[h]] - _dot(w_ref[rs, hs[h]], st[h], HI) for h in heads]
        o_st = [_dot(qd_ref[rs, hs[h]], st[h], HI) for h in heads]
        for h in heads:
            state_ref[h] = st[h] * gl[:, h:h + 1] + _dot(kdt_ref[h, ci], v_new[h], HI)
        for h in heads:
            o = o_st[h] + _dot(intra_ref[h, ci], v_new[h], HI)
            y = o * lax.rsqrt(jnp.mean(o * o, axis=-1, keepdims=True) + gain_ref[1:2, 0:1]) * gain_ref[0:1, :]
            z = z_ref[rs, hs[h]]
            o_ref[rs, hs[h]] = (y * (z * jax.nn.sigmoid(z))).astype(o_ref.dtype)


def _gdn_scan(u, w, intra, qd, kdt, gl, z, gain, nc=2):
    b, s, width = u.shape
    n = s // GDN_CHUNK
    hh = GDN_HEADS
    rows = nc * GDN_CHUNK
    tok = pl.BlockSpec((None, rows, width), lambda i, j: (i, j, 0))
    per5 = lambda d0, d1: pl.BlockSpec((None, hh, nc, d0, d1), lambda i, j: (i, 0, j, 0, 0))
    return pl.pallas_call(
        functools.partial(_gdn_scan_body, nc=nc),
        grid=(b, n // nc),
        in_specs=[tok, tok, per5(GDN_CHUNK, GDN_CHUNK), tok, per5(GDN_DH, GDN_CHUNK),
                  pl.BlockSpec((None, nc, 1, hh), lambda i, j: (i, j, 0, 0)), tok,
                  pl.BlockSpec(gain.shape, lambda i, j: (0, 0))],
        out_specs=tok,
        out_shape=jax.ShapeDtypeStruct((b, s, width), BF16),
        scratch_shapes=[pltpu.VMEM((hh, GDN_DH, GDN_DH), F32)],
        compiler_params=_cparams(("parallel", "arbitrary")),
        name="gdn_scan",
    )(u, w, intra, qd, kdt, gl, z, gain)


def _outproj_body(oa_ref, ob_ref, x_ref, wa_ref, wb_ref, g_ref, x2_ref, h2_ref):
    x2 = x_ref[...] + _dot(oa_ref[...], wa_ref[...]) + _dot(ob_ref[...], wb_ref[...])
    x2_ref[...] = x2
    r = lax.rsqrt(jnp.mean(x2 * x2, axis=-1, keepdims=True) + EPS)
    h2_ref[...] = (x2 * r * g_ref[...]).astype(BF16)


def _outproj(oa, ob, x2d, wa, wb, g, tm=512):
    t = x2d.shape[0]
    row = lambda n: pl.BlockSpec((tm, n), lambda i: (i, 0))
    full = lambda shape: pl.BlockSpec(shape, lambda i: (0,) * len(shape))
    return pl.pallas_call(
        _outproj_body,
        grid=(t // tm,),
        in_specs=[row(oa.shape[1]), row(ob.shape[1]), row(D_MODEL), full(wa.shape), full(wb.shape), full((1, D_MODEL))],
        out_specs=[row(D_MODEL), row(D_MODEL)],
        out_shape=[jax.ShapeDtypeStruct((t, D_MODEL), F32), jax.ShapeDtypeStruct((t, D_MODEL), BF16)],
        compiler_params=_cparams(("parallel",)),
        name="outproj",
    )(oa, ob, x2d, wa, wb, g)


def _top_ranks(s, k):
    n = s.shape[0]
    rows = _iota(s.shape, 0)
    rank = jnp.full(s.shape, float(k), F32)
    vals = []
    for r in range(k):
        m = jnp.max(s, axis=0, keepdims=True)
        idx = jnp.min(jnp.where(s == m, rows, n), axis=0, keepdims=True)
        hit = rows == idx
        rank = jnp.where(hit, float(r), rank)
        s = jnp.where(hit, -jnp.inf, s)
        vals.append(m)
    return rank, vals


def _route_body(h_ref, wq_ref, sk_ref, r2_ref, e2_ref, lam_ref, w1_ref):
    tm = h_ref.shape[1]
    kk = PEER_TOPK
    half = PEER_DKEY // 2
    qt = _dot(wq_ref[...], h_ref[...])
    ridx = _iota((80, tm), 0)
    grp = ridx // 8
    sub = ridx % 8
    pos = jnp.where(grp <= 1, ridx, jnp.where(grp <= 8, (grp - 1) * kk + sub, (8 + sub) * kk))
    for h in range(PEER_HEADS):
        s1 = _dot(sk_ref[0], qt[h * PEER_DKEY:h * PEER_DKEY + half], HI)
        s2 = _dot(sk_ref[1], qt[h * PEER_DKEY + half:(h + 1) * PEER_DKEY], HI)
        rank1, v1 = _top_ranks(s1, kk)
        rank2, v2 = _top_ranks(s2, kk)
        v2lo = jnp.concatenate(v2[0:8], axis=0)
        v2hi = jnp.concatenate(v2[8:16], axis=0)
        v1hi = jnp.concatenate(v1[8:16], axis=0)
        cand = jnp.concatenate([v1[0] + v2lo, v1[0] + v2hi] + [v1[r] + v2lo for r in range(1, 8)] + [v1hi + v2[0]],
                               axis=0)
        cmax = v1[0] + v2[0]
        z = jnp.zeros((1, tm), F32)
        chosen = jnp.zeros((80, tm), F32)
        for _ in range(kk):
            m = jnp.max(cand, axis=0, keepdims=True)
            p = jnp.min(jnp.where(cand == m, pos, 4 * kk * kk), axis=0, keepdims=True)
            hit = pos == p
            chosen = jnp.where(hit, 1.0, chosen)
            cand = jnp.where(hit, -jnp.inf, cand)
            z = z + jnp.exp(m - cmax)
        cnt = [jnp.sum(chosen[0:16], axis=0, keepdims=True)]
        cnt += [jnp.sum(chosen[8 * (r + 1):8 * (r + 2)], axis=0, keepdims=True) for r in range(1, 8)]
        cnt += [chosen[72 + r:73 + r] for r in range(8)]
        lam = jnp.zeros((N_KEYS, tm), F32)
        for r in range(kk):
            lam = jnp.where(rank1 == float(r), cnt[r], lam)
        r2_ref[h] = rank2.astype(BF16)
        e2_ref[h] = jnp.exp(s2 - v2[0]).astype(BF16)
        lam_ref[h] = lam
        w1_ref[h] = jnp.exp(s1 - v1[0]) / z


def _route(h2_t, wq_t, subkeys, tm=256):
    t = h2_t.shape[1]
    full = lambda shape: pl.BlockSpec(shape, lambda i: (0,) * len(shape))
    out = pl.BlockSpec((PEER_HEADS, N_KEYS, tm), lambda i: (0, 0, i))
    return pl.pallas_call(
        _route_body,
        grid=(t // tm,),
        in_specs=[pl.BlockSpec((D_MODEL, tm), lambda i: (0, i)), full(wq_t.shape), full(subkeys.shape)],
        out_specs=[out] * 4,
        out_shape=[jax.ShapeDtypeStruct((PEER_HEADS, N_KEYS, t), dt) for dt in (BF16, BF16, F32, F32)],
        compiler_params=_cparams(("parallel",)),
        name="route",
    )(h2_t, wq_t, subkeys)


def _peer_body(h_ref, u_ref, vt_ref, r2_ref, e2_ref, lam_ref, w1_ref, x2_ref, o_ref, acc_ref, coef_ref, *, sub,
               esub):
    te = u_ref.shape[0]
    tm = h_ref.shape[1]
    e = pl.program_id(1)

    @pl.when(e == 0)
    def _():
        acc_ref[...] = jnp.zeros_like(acc_ref)

    n1 = te // N_KEYS
    i1_lo = pl.multiple_of(e * n1, n1)
    hq = h_ref[...]
    act = jax.nn.gelu(_dot(u_ref[...], hq))
    per = esub // N_KEYS
    for s in range(te // esub):
        es = slice(s * esub, (s + 1) * esub)
        for kk in range(per):
            k = s * per + kk
            ks = slice(k * N_KEYS, (k + 1) * N_KEYS)
            for c in range(tm // sub):
                cs = slice(c * sub, (c + 1) * sub)
                g = jnp.zeros((N_KEYS, sub), BF16)
                for h in range(PEER_HEADS):
                    lam = jnp.broadcast_to(lam_ref[h, pl.ds(i1_lo, n1), cs][k:k + 1].astype(BF16), (N_KEYS, sub))
                    w1 = jnp.broadcast_to(w1_ref[h, pl.ds(i1_lo, n1), cs][k:k + 1].astype(BF16), (N_KEYS, sub))
                    g = g + jnp.where(r2_ref[h, :, cs] < lam, e2_ref[h, :, cs], jnp.zeros_like(w1)) * w1
                coef_ref[ks, cs] = act[ks, cs].astype(BF16) * g
        acc_ref[...] += _dot(vt_ref[:, es], coef_ref[es, :])

    @pl.when(e == pl.num_programs(1) - 1)
    def _():
        o_ref[...] = x2_ref[...] + acc_ref[...].T


def _peer(h2_t, u, vt, r2, e2, lam, w1, x2, tm=512, te=1024, sub=128, esub=256):
    t = h2_t.shape[1]
    n_exp = u.shape[0]
    rt = pl.BlockSpec((PEER_HEADS, N_KEYS, tm), lambda i, j: (0, 0, i))
    return pl.pallas_call(
        functools.partial(_peer_body, sub=sub, esub=esub),
        grid=(t // tm, n_exp // te),
        in_specs=[pl.BlockSpec((D_MODEL, tm), lambda i, j: (0, i)),
                  pl.BlockSpec((te, D_MODEL), lambda i, j: (j, 0)),
                  pl.BlockSpec((D_MODEL, te), lambda i, j: (0, j)),
                  rt, rt, rt, rt,
                  pl.BlockSpec((tm, D_MODEL), lambda i, j: (i, 0))],
        out_specs=pl.BlockSpec((tm, D_MODEL), lambda i, j: (i, 0)),
        out_shape=jax.ShapeDtypeStruct((t, D_MODEL), F32),
        scratch_shapes=[pltpu.VMEM((D_MODEL, tm), F32), pltpu.VMEM((te, tm), BF16)],
        compiler_params=_cparams(("parallel", "arbitrary")),
        name="peer",
    )(h2_t, u, vt, r2, e2, lam, w1, x2)


def _rope_tables(pos):
    half = NSA_DH // 2
    inv = jnp.exp(_real_const(-math.log(ROPE_THETA) * 2.0) * jnp.arange(half, dtype=F32) / NSA_DH)
    ang = pos.astype(F32)[:, None] * inv[None, :]
    cos, sin = jnp.cos(ang), jnp.sin(ang)
    return jnp.concatenate([cos, cos], axis=-1), jnp.concatenate([-sin, sin], axis=-1)


def _swap_halves(a):
    half = a.shape[-1] // 2
    return jnp.concatenate([a[..., half:], a[..., :half]], axis=-1)


def _importance_matrix(n_cmp_pad, n_blk):
    r = SEL_BLK // CMP_STRIDE
    l = CMP_BLK // CMP_STRIDE
    c = jnp.arange(n_cmp_pad)[:, None]
    j = jnp.arange(n_blk)[None, :]
    o = c - r * j
    wgt = jnp.minimum(o + l, r) - jnp.maximum(o, 0)
    return jnp.where((o >= -(l - 1)) & (o < r), wgt, 0).astype(F32)


def _layer(x, attn_norm, w_in, q_gain, k_gain, cmp_pos, cmp_w1, cmp_w2, conv_w, a_log, dt_bias, out_gain,
           w_out, ffn_norm, w_query, subkeys, u_tab, v_tab):
    b, s, d = x.shape
    t = b * s
    x2d = x.reshape(t, d)
    nw = NSA_HEADS * NSA_DH
    nkv = NSA_GROUPS * NSA_DH
    gw = GDN_HEADS * GDN_DH

    o_q, o_kv, o_gate = 0, nw, nw + 6 * nkv
    o_qkvb = o_gate + 3 * NSA_HEADS
    o_z = o_qkvb + 3 * gw
    o_a = o_z + gw
    o_b = o_a + GDN_HEADS
    n_small = 3 * NSA_HEADS + 2 * GDN_HEADS
    w_perm = jnp.concatenate([w_in[:, o_q:o_gate], w_in[:, o_qkvb:o_a], w_in[:, o_gate:o_qkvb], w_in[:, o_a:],
                              jnp.zeros((d, LANES - n_small), w_in.dtype)], axis=1).astype(BF16)
    seg = jnp.kron(jnp.eye(LANES // NSA_DH, dtype=F32), jnp.full((NSA_DH, NSA_DH), 1.0 / NSA_DH, F32))
    cos, sin = _rope_tables(jnp.arange(s))
    cos2, sin2 = jnp.tile(cos, (1, 2)), jnp.tile(sin, (1, 2))
    qg = jnp.tile(q_gain, 2).reshape(1, LANES)
    kg = jnp.tile(k_gain, (1, 2))
    q_a, kv_a, qkv_b, z_b, small = _proj(x2d, attn_norm.reshape(1, d), w_perm, seg, cos2, sin2, qg, kg, s)

    def heads(a):
        return a.reshape(b, s, NSA_GROUPS, NSA_DH).transpose(0, 2, 1, 3)

    kv = [heads(kv_a[:, i * nkv:(i + 1) * nkv]) for i in range(6)]
    n16 = s // CMP_STRIDE
    a16k = kv[0].reshape(b, NSA_GROUPS, n16, CMP_STRIDE * NSA_DH)
    a16v = kv[1].reshape(b, NSA_GROUPS, n16, CMP_STRIDE * NSA_DH)
    cmp_end = jnp.arange(n16) * CMP_STRIDE + CMP_BLK - 1
    ccos, csin = _rope_tables(cmp_end)
    pos8 = jnp.broadcast_to(cmp_pos.reshape(2, 1, CMP_BLK * NSA_DH), (2, 8, CMP_BLK * NSA_DH)).astype(BF16)
    gelu_consts = jnp.stack([_real_const(math.sqrt(2.0 / math.pi)), _real_const(0.044715)]).reshape(1, 2)
    kc, vc = _compress(a16k, a16v, cmp_w1.astype(BF16), cmp_w2.astype(BF16), _swap_halves(cmp_w2[0]).astype(BF16),
                       pos8, k_gain[0:1], _swap_halves(k_gain[0:1]), ccos, csin, gelu_consts)
    n_qb = s // Q_BLK
    n_blk = s // SEL_BLK
    nq = NSA_HPG * Q_BLK
    q_t = q_a.reshape(b, n_qb, Q_BLK, NSA_GROUPS, NSA_HPG, NSA_DH).transpose(0, 3, 1, 5, 4, 2)
    q_t = q_t.reshape(b, NSA_GROUPS, n_qb, NSA_DH, nq)

    def tiles_t(a, tile):
        return a.astype(BF16).reshape(b, NSA_GROUPS, s // tile, tile, NSA_DH).transpose(0, 1, 2, 4, 3)
    gate = small[:, 0:3 * NSA_HEADS].reshape(b, n_qb, Q_BLK, NSA_GROUPS, NSA_HPG, 3)
    gate = gate.transpose(0, 3, 1, 5, 4, 2).reshape(b, NSA_GROUPS, n_qb, 3, nq)
    wimp_t = _importance_matrix(n16, n_blk).T
    own_blk = jnp.arange(s)[:, None] // SEL_BLK == jnp.arange(n_blk)[None, :]
    off = jnp.broadcast_to(jnp.where(own_blk, -MASK_OFF, 0.0).astype(BF16), (b, NSA_GROUPS, s, n_blk))
    ke = jnp.concatenate([off, kv[2].astype(BF16)], axis=-1)
    o_t = _nsa(q_t, kc, vc.transpose(0, 1, 3, 2), ke, tiles_t(kv[3], KEY_TILE), kv[4].astype(BF16),
               tiles_t(kv[5], Q_BLK), gate, wimp_t)
    o_nsa = o_t.reshape(b, NSA_GROUPS, n_qb, NSA_DH, NSA_HPG, Q_BLK).transpose(0, 2, 5, 1, 4, 3).reshape(t, nw)

    n_chunk = s // GDN_CHUNK
    eps_row = jnp.broadcast_to(_real_const(EPS), (1, GDN_DH))
    q_scale = jnp.concatenate([jnp.broadcast_to(_real_const(GDN_DH ** -0.5), (1, GDN_DH)), eps_row], axis=0)
    gq, gk, gv = _gdn_prep(qkv_b.reshape(b, s, 3 * gw), conv_w, q_scale)
    kt5 = gk.reshape(b, n_chunk, GDN_CHUNK, GDN_HEADS, GDN_DH).transpose(0, 3, 1, 4, 2)
    a_col = small[:, 3 * NSA_HEADS:3 * NSA_HEADS + GDN_HEADS].reshape(b, s, GDN_HEADS)
    b_col = small[:, 3 * NSA_HEADS + GDN_HEADS:n_small].reshape(b, s, GDN_HEADS)
    u, w, intra, qd, kdt, gl = _gdn_local(gq, gk, kt5, gv, a_col, b_col, a_col.transpose(0, 2, 1), a_log, dt_bias)
    gain_eps = jnp.concatenate([out_gain.reshape(1, GDN_DH), eps_row], axis=0)
    o_gdn = _gdn_scan(u, w, intra, qd, kdt, gl, z_b.reshape(b, s, gw), gain_eps).reshape(t, gw)

    w_out_b = w_out.astype(BF16)
    x2, h2 = _outproj(o_nsa, o_gdn, x2d, w_out_b[:nw], w_out_b[nw:], ffn_norm.reshape(1, d))

    h2_t = h2.T
    r2, e2, lam, w1 = _route(h2_t, w_query.T.astype(BF16), subkeys)
    out = _peer(h2_t, u_tab.astype(BF16), v_tab.T.astype(BF16), r2, e2, lam, w1, x2)
    return out.reshape(b, s, d)


def kernel(x, attn_norm, w_in, nsa_q_gain, nsa_k_gain, cmp_pos, cmp_w1, cmp_w2, gdn_conv, gdn_a_log, gdn_dt_bias,
           gdn_out_gain, w_out, ffn_norm, peer_w_query, peer_subkeys, peer_u, peer_v):
    for l in range(attn_norm.shape[0]):
        x = _layer(x, attn_norm[l], w_in[l], nsa_q_gain[l], nsa_k_gain[l], cmp_pos[l], cmp_w1[l], cmp_w2[l],
                   gdn_conv[l], gdn_a_log[l], gdn_dt_bias[l], gdn_out_gain[l], w_out[l], ffn_norm[l],
                   peer_w_query[l], peer_subkeys[l], peer_u[l], peer_v[l])
    return x
```

```python
import functools
import math

import jax
import jax.numpy as jnp
import numpy as np
from jax import lax
from jax.experimental import pallas as pl
from jax.experimental.pallas import tpu as pltpu

F32 = jnp.float32
BF16 = jnp.bfloat16
HI = lax.Precision.HIGHEST

D_MODEL = 1024
NSA_HEADS = 8
NSA_GROUPS = 2
NSA_HPG = NSA_HEADS // NSA_GROUPS
NSA_DH = 64
CMP_BLK = 32
CMP_STRIDE = 16
CMP_HID = 256
SEL_BLK = 64
N_SEL = 16
WINDOW = 512
Q_BLK = 128
ROPE_THETA = 10000.0
GDN_HEADS = 4
GDN_DH = 128
CONV_K = 4
GDN_CHUNK = 64
PEER_HEADS = 8
PEER_TOPK = 16
N_KEYS = 128
PEER_DKEY = 256
NEG = -1e30
MASK_OFF = 1e30
FORCE = 1e4
EPS = 1e-6

KEY_TILE = 1024
LANES = 128
VMEM_LIMIT = 56 * 1024 * 1024


def _cparams(sem):
    return pltpu.CompilerParams(dimension_semantics=sem, vmem_limit_bytes=VMEM_LIMIT)


def _dot(a, b, precision=None):
    return jnp.dot(a, b, preferred_element_type=F32, precision=precision)


def _split(a):
    hi = a.astype(BF16)
    return hi, (a - hi.astype(F32)).astype(BF16)


def _dot3(a, b):
    (a_hi, a_lo), (b_hi, b_lo) = a, b
    return _dot(a_hi, b_hi) + (_dot(a_hi, b_lo) + _dot(a_lo, b_hi))


def _dot_nt(a, b, precision=None):
    return lax.dot_general(a, b, (((1,), (1,)), ((), ())), preferred_element_type=F32, precision=precision)


def _dot_tn(a, b):
    return lax.dot_general(a, b, (((0,), (0,)), ((), ())), preferred_element_type=F32)


def _iota(shape, axis):
    return lax.broadcasted_iota(jnp.int32, shape, axis)


def _real_const(c):
    hi = np.float32(c)
    lo = np.float32(c - float(hi))
    return jnp.asarray(hi) + jnp.asarray(lo)


def _gelu_tanh(x, c0, c1):
    return x * (0.5 * (1.0 + jnp.tanh(c0 * (x + c1 * (x * x * x)))))


def _pair_norm_rope(x, seg, gain, cos, sinsg):
    ms = _dot(x * x, seg, HI)
    y = x * lax.rsqrt(ms + EPS) * gain
    lane = _iota(y.shape, 1) % NSA_DH
    swapped = jnp.where(lane < NSA_DH // 2, pltpu.roll(y, LANES - NSA_DH // 2, 1), pltpu.roll(y, NSA_DH // 2, 1))
    return y * cos + swapped * sinsg


def _proj_body(x_ref, g_ref, w_ref, seg_ref, cos_ref, sin_ref, qg_ref, kg_ref,
               q_ref, kv_ref, qkvb_ref, zb_ref, sm_ref):
    x = x_ref[...]
    r = lax.rsqrt(jnp.mean(x * x, axis=-1, keepdims=True) + EPS)
    h = (x * r * g_ref[...]).astype(BF16)
    seg = seg_ref[...]
    cos = cos_ref[...]
    sin = sin_ref[...]
    scale = NSA_DH ** -0.5
    for j in range(4):
        p = _dot(h, w_ref[:, j * LANES:(j + 1) * LANES])
        q_ref[:, j * LANES:(j + 1) * LANES] = (_pair_norm_rope(p, seg, qg_ref[...], cos, sin) * scale).astype(BF16)
    for j in range(6):
        p = _dot(h, w_ref[:, 512 + j * LANES:512 + (j + 1) * LANES])
        if j == 2:
            p = _pair_norm_rope(p, seg, kg_ref[1:2, :], cos, sin)
        elif j == 4:
            p = _pair_norm_rope(p, seg, kg_ref[2:3, :], cos, sin)
        kv_ref[:, j * LANES:(j + 1) * LANES] = p
    qkvb_ref[...] = _dot(h, w_ref[:, 1280:2816])
    zb_ref[...] = _dot(h, w_ref[:, 2816:3328])
    sm_ref[...] = _dot(h, w_ref[:, 3328:3456])


def _proj(x2d, g, w, seg, cos, sin, qg, kg, seq, tm=256):
    t = x2d.shape[0]
    nps = seq // tm
    full = lambda shape: pl.BlockSpec(shape, lambda i: (0,) * len(shape))
    row = lambda n: pl.BlockSpec((tm, n), lambda i: (i, 0))
    return pl.pallas_call(
        _proj_body,
        grid=(t // tm,),
        in_specs=[row(D_MODEL), full((1, D_MODEL)), full(w.shape), full((LANES, LANES)),
                  pl.BlockSpec((tm, LANES), lambda i: (i % nps, 0)),
                  pl.BlockSpec((tm, LANES), lambda i: (i % nps, 0)),
                  full((1, LANES)), full((3, LANES))],
        out_specs=[row(512), row(768), row(1536), row(512), row(LANES)],
        out_shape=[jax.ShapeDtypeStruct((t, 512), BF16), jax.ShapeDtypeStruct((t, 768), F32),
                   jax.ShapeDtypeStruct((t, 1536), F32), jax.ShapeDtypeStruct((t, 512), F32),
                   jax.ShapeDtypeStruct((t, LANES), F32)],
        compiler_params=_cparams(("parallel",)),
        name="proj",
    )(x2d, g, w, seg, cos, sin, qg, kg)


def _compress_body(ak_ref, av_ref, w1_ref, w2_ref, w2p_ref, pos_ref, gain_ref, gainp_ref, cos_ref, sin_ref, gc_ref,
                   kc_ref, vc_ref):
    n = ak_ref.shape[0]
    half = ak_ref.shape[1]

    def hidden(a_ref, j):
        a = a_ref[...].astype(BF16)
        ha = _dot(a, w1_ref[j, 0:half, :])
        hb = _dot(a, w1_ref[j, half:2 * half, :])
        pb = _dot(pos_ref[j], w1_ref[j])
        hid = ha + pltpu.roll(hb, n - 1, 0) + pb[0:1, :]
        return _gelu_tanh(hid, gc_ref[0:1, 0:1], gc_ref[0:1, 1:2]).astype(BF16)

    hk = hidden(ak_ref, 0)
    ck = _dot(hk, w2_ref[0])
    ckp = _dot(hk, w2p_ref[...])
    r = lax.rsqrt(jnp.mean(ck * ck, axis=-1, keepdims=True) + EPS)
    kc = ck * r * gain_ref[...] * cos_ref[...] + ckp * r * gainp_ref[...] * sin_ref[...]
    kc_ref[...] = kc.astype(BF16)
    hv = hidden(av_ref, 1)
    vc_ref[...] = _dot(hv, w2_ref[1]).astype(BF16)


def _compress(a16k, a16v, w1, w2, w2p, pos8, gain, gainp, cos, sin, gelu_consts):
    b, g, n, half = a16k.shape
    full = lambda shape: pl.BlockSpec(shape, lambda i, j: (0,) * len(shape))
    per = lambda last: pl.BlockSpec((None, None, n, last), lambda i, j: (i, j, 0, 0))
    return pl.pallas_call(
        _compress_body,
        grid=(b, g),
        in_specs=[per(half), per(half), full(w1.shape), full(w2.shape), full(w2p.shape), full(pos8.shape),
                  full(gain.shape), full(gainp.shape), full(cos.shape), full(sin.shape), full(gelu_consts.shape)],
        out_specs=[per(NSA_DH), per(NSA_DH)],
        out_shape=[jax.ShapeDtypeStruct((b, g, n, NSA_DH), BF16)] * 2,
        compiler_params=_cparams(("parallel", "parallel")),
        name="compress",
    )(a16k, a16v, w1, w2, w2p, pos8, gain, gainp, cos, sin, gelu_consts)


def _col_softmax(s):
    e = jnp.exp(s - jnp.max(s, axis=0, keepdims=True))
    return e, jnp.sum(e, axis=0, keepdims=True)


def _nsa_body(q_ref, kc_ref, vc_ref, ke_ref, vs_ref, kw_ref, vw_ref, gate_ref, wimp_ref, o_ref, *, n_sel):
    n_cmp = kc_ref.shape[0]
    n_blk = wimp_ref.shape[0]
    nq = NSA_HPG * Q_BLK
    t0 = pl.program_id(2) * Q_BLK
    t_row = t0 + _iota((1, Q_BLK), 1)
    t_row4 = t0 + _iota((1, nq), 1) % Q_BLK
    q_t = q_ref[...]
    gates = jax.nn.sigmoid(gate_ref[...])
    lanes = lambda h: slice(h * Q_BLK, (h + 1) * Q_BLK)

    s_c = _dot(kc_ref[...], q_t)
    cmp_end = _iota((n_cmp, 1), 0) * CMP_STRIDE + (CMP_BLK - 1)
    bias_c = jnp.where(cmp_end <= t_row, 0.0, NEG)
    any_c = (t_row >= CMP_BLK - 1).astype(F32)
    vc_t = vc_ref[...]
    o_c = []
    p_sum = jnp.zeros((n_cmp, Q_BLK), F32)
    for h in range(NSA_HPG):
        e, l = _col_softmax(s_c[:, lanes(h)] + bias_c)
        p = e * (any_c / l)
        p_sum = p_sum + p
        o_c.append(_dot(vc_t, p.astype(BF16)))
    imp = _dot(wimp_ref[...], p_sum, HI)

    blk = _iota((n_blk, Q_BLK), 0)
    cur = t_row // SEL_BLK
    forced = (blk == 0) | (blk == cur) | (blk == cur - 1)
    causal = blk * SEL_BLK <= t_row
    score = jnp.where(causal, jnp.where(forced, FORCE, imp), NEG)
    sel = jnp.zeros((n_blk, Q_BLK), F32)
    for _ in range(n_sel):
        m = jnp.max(score, axis=0, keepdims=True)
        idx = jnp.min(jnp.where(score == m, blk, n_blk), axis=0, keepdims=True)
        hit = blk == idx
        sel = jnp.where(hit, 1.0, sel)
        score = jnp.where(hit, -jnp.inf, score)
    nsel = jnp.where(causal, 1.0 - sel, 1.0).astype(BF16)
    bmat = jnp.concatenate([jnp.concatenate([nsel] * NSA_HPG, axis=1), q_t], axis=0)

    def sel_tile(kt, carry, diagonal):
        m, l, acc = carry
        start = pl.multiple_of(kt * KEY_TILE, KEY_TILE)
        s = _dot(ke_ref[pl.ds(start, KEY_TILE), :], bmat)
        if diagonal:
            s = jnp.where(start + _iota((KEY_TILE, 1), 0) <= t_row4, s, NEG)
        m_new = jnp.maximum(m, jnp.max(s, axis=0, keepdims=True))
        alpha = jnp.exp(m - m_new)
        p = jnp.exp(s - m_new)
        l = alpha * l + jnp.sum(p, axis=0, keepdims=True)
        acc = alpha * acc + _dot(vs_ref[kt], p.astype(BF16))
        return m_new, l, acc

    init = (jnp.full((1, nq), NEG, F32), jnp.zeros((1, nq), F32), jnp.zeros((NSA_DH, nq), F32))
    n_full = t0 // KEY_TILE
    carry = lax.fori_loop(0, n_full, functools.partial(sel_tile, diagonal=False), init)
    _, l_s, acc_s = sel_tile(n_full, carry, True)
    o_s = acc_s / l_s

    wspan = WINDOW + Q_BLK
    wstart = pl.multiple_of(jnp.maximum(t0 - WINDOW, 0), Q_BLK)
    s_w = _dot(kw_ref[pl.ds(wstart, wspan), :], q_t)
    w0 = wstart // Q_BLK
    vwin_t = jnp.concatenate([vw_ref[w0 + i] for i in range(wspan // Q_BLK)], axis=1)
    rel = t_row - (wstart + _iota((wspan, 1), 0))
    bias_w = jnp.where((rel >= 0) & (rel < WINDOW), 0.0, NEG)

    for h in range(NSA_HPG):
        e, l = _col_softmax(s_w[:, lanes(h)] + bias_w)
        o_w = _dot(vwin_t, e.astype(BF16)) / l
        g = gates[:, lanes(h)]
        o = o_c[h] * g[0:1] + o_s[:, lanes(h)] * g[1:2] + o_w * g[2:3]
        o_ref[:, lanes(h)] = o.astype(o_ref.dtype)


def _nsa(q_t, kc, vc_t, ke, vs_t, kw, vw_t, gate, wimp):
    b, _, n_qb, _, nq = q_t.shape
    s = n_qb * Q_BLK
    n_cmp = kc.shape[2]
    n_blk = s // SEL_BLK
    per_g = lambda a: pl.BlockSpec((None, None) + a.shape[2:], lambda i, j, k: (i, j) + (0,) * (a.ndim - 2))
    per_q = lambda rows: pl.BlockSpec((None, None, None, rows, nq), lambda i, j, k: (i, j, k, 0, 0))
    return pl.pallas_call(
        functools.partial(_nsa_body, n_sel=min(N_SEL, n_blk)),
        grid=(b, NSA_GROUPS, n_qb),
        in_specs=[per_q(NSA_DH), per_g(kc), per_g(vc_t), per_g(ke), per_g(vs_t), per_g(kw), per_g(vw_t), per_q(3),
                  pl.BlockSpec(wimp.shape, lambda i, j, k: (0, 0))],
        out_specs=per_q(NSA_DH),
        out_shape=jax.ShapeDtypeStruct((b, NSA_GROUPS, n_qb, NSA_DH, nq), BF16),
        compiler_params=_cparams(("parallel", "parallel", "arbitrary")),
        name="nsa",
    )(q_t, kc, vc_t, ke, vs_t, kw, vw_t, gate, wimp)


def _gdn_prep_body(x_ref, prev_ref, w_ref, qs_ref, q_ref, k_ref, v_ref, pad_ref):
    tm = x_ref.shape[0]
    width = GDN_HEADS * GDN_DH
    first = pl.program_id(1) == 0
    pad_ref[0:8, :] = jnp.where(first, 0.0, prev_ref[...])
    pad_ref[8:8 + tm, :] = x_ref[...]
    conv = pad_ref[8:8 + tm, :] * w_ref[CONV_K - 1:CONV_K, :]
    for j in range(CONV_K - 1):
        sh = CONV_K - 1 - j
        conv = conv + pad_ref[8 - sh:8 - sh + tm, :] * w_ref[j:j + 1, :]
    conv = conv * jax.nn.sigmoid(conv)
    for part, ref in enumerate((q_ref, k_ref, v_ref)):
        for h in range(GDN_HEADS):
            lo = part * width + h * GDN_DH
            c = conv[:, lo:lo + GDN_DH]
            if part < 2:
                c = c * lax.rsqrt(jnp.sum(c * c, axis=-1, keepdims=True) + qs_ref[1:2, 0:1])
            if part == 0:
                c = c * qs_ref[0:1, :]
            ref[:, h * GDN_DH:(h + 1) * GDN_DH] = c


def _gdn_prep(qkv, conv_w, q_scale, tm=256):
    b, s, w3 = qkv.shape
    width = w3 // 3
    out = pl.BlockSpec((None, tm, width), lambda i, j: (i, j, 0))
    return pl.pallas_call(
        _gdn_prep_body,
        grid=(b, s // tm),
        in_specs=[pl.BlockSpec((None, tm, w3), lambda i, j: (i, j, 0)),
                  pl.BlockSpec((None, 8, w3), lambda i, j: (i, jnp.maximum(j * (tm // 8) - 1, 0), 0)),
                  pl.BlockSpec(conv_w.shape, lambda i, j: (0, 0)),
                  pl.BlockSpec(q_scale.shape, lambda i, j: (0, 0))],
        out_specs=[out, out, out],
        out_shape=[jax.ShapeDtypeStruct((b, s, width), F32)] * 3,
        scratch_shapes=[pltpu.VMEM((tm + 8, w3), F32)],
        compiler_params=_cparams(("parallel", "parallel")),
        name="gdn_prep",
    )(qkv, qkv, conv_w, q_scale)


def _gdn_local_body(q_ref, k_ref, kt_ref, v_ref, ac_ref, bc_ref, ar_ref, alog_c_ref, dtb_c_ref, alog_r_ref, dtb_r_ref,
                    u_ref, w_ref, intra_ref, qd_ref, kdt_ref, gl_ref, *, nc):
    c = GDN_CHUNK
    row = _iota((c, c), 0)
    col = _iota((c, c), 1)
    incl = row >= col
    strict = row > col
    tri = incl.astype(F32)
    eye = (row == col).astype(F32)
    xs, ts, rhs_u, rhs_w, where_to = [], [], [], [], []
    for ci in range(nc):
        rs = slice(ci * c, (ci + 1) * c)
        g_c = -jnp.exp(alog_c_ref[...]) * jax.nn.softplus(ac_ref[rs, :] + dtb_c_ref[...])
        g_r = -jnp.exp(alog_r_ref[...]) * jax.nn.softplus(ar_ref[:, rs] + dtb_r_ref[...])
        gc_all = _dot(tri, g_c, HI)
        gr_all = _dot_nt(g_r, tri, HI)
        beta_all = jax.nn.sigmoid(bc_ref[rs, :])
        gl_ref[ci] = jnp.exp(gc_all[c - 1:c, :])
        for h in range(GDN_HEADS):
            hs = slice(h * GDN_DH, (h + 1) * GDN_DH)
            gc = gc_all[:, h:h + 1]
            gr = gr_all[h:h + 1, :]
            g_last = gc_all[c - 1:c, h:h + 1]
            beta = beta_all[:, h:h + 1]
            q = q_ref[rs, hs]
            kt = kt_ref[h, ci]
            kb = k_ref[rs, hs] * beta
            decay = jnp.where(incl, jnp.exp(jnp.where(incl, gc - gr, 0.0)), 0.0)
            a = jnp.where(strict, _dot(kb, kt) * decay, 0.0)
            eg = jnp.exp(gc)
            xs.append(_split(-a))
            ts.append(eye - a)
            rhs_u.append(v_ref[rs, hs] * beta)
            rhs_w.append(kb * eg)
            where_to.append((rs, hs))
            intra_ref[h, ci] = jnp.where(incl, _dot(q, kt) * decay, 0.0)
            qd_ref[rs, hs] = q * eg
            kdt_ref[h, ci] = kt * jnp.exp(g_last - gr)
    for _ in range(int(math.log2(c)) - 1):
        xs = [_dot3(x, x) for x in xs]
        xs = [_split(x) for x in xs]
        ts = [t + _dot3(_split(t), x) for t, x in zip(ts, xs)]
    for t, ru, rw, (rs, hs) in zip(ts, rhs_u, rhs_w, where_to):
        t = _split(t)
        u_ref[rs, hs] = _dot3(t, _split(ru))
        w_ref[rs, hs] = _dot3(t, _split(rw))


def _gdn_local(q, k, kt5, v, a_col, b_col, a_row, alog, dtb, nc=2):
    b, s, width = q.shape
    n = s // GDN_CHUNK
    hh = GDN_HEADS
    rows = nc * GDN_CHUNK
    tok = pl.BlockSpec((None, rows, width), lambda i, j: (i, j, 0))
    per5 = lambda d0, d1: pl.BlockSpec((None, hh, nc, d0, d1), lambda i, j: (i, 0, j, 0, 0))
    small = lambda shape: pl.BlockSpec(shape, lambda i, j: (0,) * len(shape))
    return pl.pallas_call(
        functools.partial(_gdn_local_body, nc=nc),
        grid=(b, n // nc),
        in_specs=[tok, tok, per5(GDN_DH, GDN_CHUNK), tok,
                  pl.BlockSpec((None, rows, hh), lambda i, j: (i, j, 0)),
                  pl.BlockSpec((None, rows, hh), lambda i, j: (i, j, 0)),
                  pl.BlockSpec((None, hh, rows), lambda i, j: (i, 0, j)),
                  small((1, hh)), small((1, hh)), small((hh, 1)), small((hh, 1))],
        out_specs=[tok, tok, per5(GDN_CHUNK, GDN_CHUNK), tok, per5(GDN_DH, GDN_CHUNK),
                   pl.BlockSpec((None, nc, 1, hh), lambda i, j: (i, j, 0, 0))],
        out_shape=[jax.ShapeDtypeStruct((b, s, width), F32), jax.ShapeDtypeStruct((b, s, width), F32),
                   jax.ShapeDtypeStruct((b, hh, n, GDN_CHUNK, GDN_CHUNK), F32),
                   jax.ShapeDtypeStruct((b, s, width), F32),
                   jax.ShapeDtypeStruct((b, hh, n, GDN_DH, GDN_CHUNK), F32),
                   jax.ShapeDtypeStruct((b, n, 1, hh), F32)],
        compiler_params=_cparams(("parallel", "parallel")),
        name="gdn_local",
    )(q, k, kt5, v, a_col, b_col, a_row, alog.reshape(1, hh), dtb.reshape(1, hh), alog.reshape(hh, 1),
      dtb.reshape(hh, 1))


def _gdn_scan_body(u_ref, w_ref, intra_ref, qd_ref, kdt_ref, gl_ref, z_ref, gain_ref, o_ref, state_ref, *, nc):
    c = GDN_CHUNK

    @pl.when(pl.program_id(1) == 0)
    def _():
        state_ref[...] = jnp.zeros_like(state_ref)

    heads = range(GDN_HEADS)
    hs = [slice(h * GDN_DH, (h + 1) * GDN_DH) for h in heads]
    for ci in range(nc):
        rs = slice(ci * c, (ci + 1) * c)
        gl = gl_ref[ci]
        st = [state_ref[h] for h in heads]
        st2 = [_split(s) for s in st]
        v_new = [u_ref[rs, hs[h]] - _dot3(_split(w_ref[rs, hs[h]]), st2[h]) for h in heads]
        o_st = [_dot3(_split(qd_ref[rs, hs[h]]), st2[h]) for h in heads]
        vn2 = [_split(v) for v in v_new]
        for h in heads:
            state_ref[h] = st[h] * gl[:, h:h + 1] + _dot3(_split(kdt_ref[h, ci]), vn2[h])
        for h in heads:
            o = o_st[h] + _dot3(_split(intra_ref[h, ci]), vn2[h])
            y = o * lax.rsqrt(jnp.mean(o * o, axis=-1, keepdims=True) + gain_ref[1:2, 0:1]) * gain_ref[0:1, :]
            z = z_ref[rs, hs[h]]
            o_ref[rs, hs[h]] = (y * (z * jax.nn.sigmoid(z))).astype(o_ref.dtype)


def _gdn_scan(u, w, intra, qd, kdt, gl, z, gain, nc=2):
    b, s, width = u.shape
    n = s // GDN_CHUNK
    hh = GDN_HEADS
    rows = nc * GDN_CHUNK
    tok = pl.BlockSpec((None, rows, width), lambda i, j: (i, j, 0))
    per5 = lambda d0, d1: pl.BlockSpec((None, hh, nc, d0, d1), lambda i, j: (i, 0, j, 0, 0))
    return pl.pallas_call(
        functools.partial(_gdn_scan_body, nc=nc),
        grid=(b, n // nc),
        in_specs=[tok, tok, per5(GDN_CHUNK, GDN_CHUNK), tok, per5(GDN_DH, GDN_CHUNK),
                  pl.BlockSpec((None, nc, 1, hh), lambda i, j: (i, j, 0, 0)), tok,
                  pl.BlockSpec(gain.shape, lambda i, j: (0, 0))],
        out_specs=tok,
        out_shape=jax.ShapeDtypeStruct((b, s, width), BF16),
        scratch_shapes=[pltpu.VMEM((hh, GDN_DH, GDN_DH), F32)],
        compiler_params=_cparams(("parallel", "arbitrary")),
        name="gdn_scan",
    )(u, w, intra, qd, kdt, gl, z, gain)


def _outproj_body(oa_ref, ob_ref, x_ref, wa_ref, wb_ref, g_ref, x2_ref, h2_ref):
    x2 = x_ref[...] + _dot(oa_ref[...], wa_ref[...]) + _dot(ob_ref[...], wb_ref[...])
    x2_ref[...] = x2
    r = lax.rsqrt(jnp.mean(x2 * x2, axis=-1, keepdims=True) + EPS)
    h2_ref[...] = (x2 * r * g_ref[...]).astype(BF16)


def _outproj(oa, ob, x2d, wa, wb, g, tm=512):
    t = x2d.shape[0]
    row = lambda n: pl.BlockSpec((tm, n), lambda i: (i, 0))
    full = lambda shape: pl.BlockSpec(shape, lambda i: (0,) * len(shape))
    return pl.pallas_call(
        _outproj_body,
        grid=(t // tm,),
        in_specs=[row(oa.shape[1]), row(ob.shape[1]), row(D_MODEL), full(wa.shape), full(wb.shape), full((1, D_MODEL))],
        out_specs=[row(D_MODEL), row(D_MODEL)],
        out_shape=[jax.ShapeDtypeStruct((t, D_MODEL), F32), jax.ShapeDtypeStruct((t, D_MODEL), BF16)],
        compiler_params=_cparams(("parallel",)),
        name="outproj",
    )(oa, ob, x2d, wa, wb, g)


def _top_ranks(s, k, want_rank):
    n = s.shape[0]
    rows = _iota(s.shape, 0)
    rank = jnp.full(s.shape, float(k), F32) if want_rank else None
    vals, idxs = [], []
    for r in range(k):
        m = jnp.max(s, axis=0, keepdims=True)
        idx = jnp.min(jnp.where(s == m, rows, n), axis=0, keepdims=True)
        hit = rows == idx
        if want_rank:
            rank = jnp.where(hit, float(r), rank)
        s = jnp.where(hit, -jnp.inf, s)
        vals.append(m)
        idxs.append(idx)
    return rank, vals, idxs


def _route_body(h_ref, wq_ref, sk_ref, r2_ref, e2_ref, lam_ref, w1_ref):
    tm = h_ref.shape[1]
    kk = PEER_TOPK
    half = PEER_DKEY // 2
    qt = _dot(wq_ref[...], h_ref[...])
    ridx = _iota((80, tm), 0)
    grp = ridx // 8
    sub = ridx % 8
    pos = jnp.where(grp <= 1, ridx, jnp.where(grp <= 8, (grp - 1) * kk + sub, (8 + sub) * kk))
    for h in range(PEER_HEADS):
        s1 = _dot(sk_ref[0], qt[h * PEER_DKEY:h * PEER_DKEY + half], HI)
        s2 = _dot(sk_ref[1], qt[h * PEER_DKEY + half:(h + 1) * PEER_DKEY], HI)
        _, v1, idx1 = _top_ranks(s1, kk, False)
        rank2, v2, _ = _top_ranks(s2, kk, True)
        v2lo = jnp.concatenate(v2[0:8], axis=0)
        v2hi = jnp.concatenate(v2[8:16], axis=0)
        v1hi = jnp.concatenate(v1[8:16], axis=0)
        cand = jnp.concatenate([v1[0] + v2lo, v1[0] + v2hi] + [v1[r] + v2lo for r in range(1, 8)] + [v1hi + v2[0]],
                               axis=0)
        cmax = v1[0] + v2[0]
        z = jnp.zeros((1, tm), F32)
        chosen = jnp.zeros((80, tm), F32)
        for _ in range(kk):
            m = jnp.max(cand, axis=0, keepdims=True)
            p = jnp.min(jnp.where(cand == m, pos, 4 * kk * kk), axis=0, keepdims=True)
            hit = pos == p
            chosen = jnp.where(hit, 1.0, chosen)
            cand = jnp.where(hit, -jnp.inf, cand)
            z = z + jnp.exp(m - cmax)
        cnt = [jnp.sum(chosen[0:16], axis=0, keepdims=True)]
        cnt += [jnp.sum(chosen[8 * (r + 1):8 * (r + 2)], axis=0, keepdims=True) for r in range(1, 8)]
        cnt += [chosen[72 + r:73 + r] for r in range(8)]
        lam = jnp.zeros((N_KEYS, tm), F32)
        keys = _iota((N_KEYS, tm), 0)
        for r in range(kk):
            lam = jnp.where(keys == idx1[r], cnt[r], lam)
        r2_ref[h] = rank2.astype(BF16)
        e2_ref[h] = jnp.exp(s2 - v2[0]).astype(BF16)
        lam_ref[h] = lam
        w1_ref[h] = jnp.exp(s1 - v1[0]) / z


def _route(h2_t, wq_t, subkeys, tm=256):
    t = h2_t.shape[1]
    full = lambda shape: pl.BlockSpec(shape, lambda i: (0,) * len(shape))
    out = pl.BlockSpec((PEER_HEADS, N_KEYS, tm), lambda i: (0, 0, i))
    return pl.pallas_call(
        _route_body,
        grid=(t // tm,),
        in_specs=[pl.BlockSpec((D_MODEL, tm), lambda i: (0, i)), full(wq_t.shape), full(subkeys.shape)],
        out_specs=[out] * 4,
        out_shape=[jax.ShapeDtypeStruct((PEER_HEADS, N_KEYS, t), dt) for dt in (BF16, BF16, F32, F32)],
        compiler_params=_cparams(("parallel",)),
        name="route",
    )(h2_t, wq_t, subkeys)


def _peer_body(h_ref, u_ref, vt_ref, r2_ref, e2_ref, lam_ref, w1_ref, x2_ref, o_ref, acc_ref, coef_ref, *, sub,
               esub):
    te = u_ref.shape[0]
    tm = h_ref.shape[1]
    e = pl.program_id(1)

    @pl.when(e == 0)
    def _():
        acc_ref[...] = jnp.zeros_like(acc_ref)

    n1 = te // N_KEYS
    i1_lo = pl.multiple_of(e * n1, n1)
    hq = h_ref[...]
    act = jax.nn.gelu(_dot(u_ref[...], hq))
    per = esub // N_KEYS
    for s in range(te // esub):
        es = slice(s * esub, (s + 1) * esub)
        for kk in range(per):
            k = s * per + kk
            ks = slice(k * N_KEYS, (k + 1) * N_KEYS)
            for c in range(tm // sub):
                cs = slice(c * sub, (c + 1) * sub)
                g = jnp.zeros((N_KEYS, sub), BF16)
                for h in range(PEER_HEADS):
                    lam = jnp.broadcast_to(lam_ref[h, pl.ds(i1_lo, n1), cs][k:k + 1].astype(BF16), (N_KEYS, sub))
                    w1 = jnp.broadcast_to(w1_ref[h, pl.ds(i1_lo, n1), cs][k:k + 1].astype(BF16), (N_KEYS, sub))
                    g = g + jnp.where(r2_ref[h, :, cs] < lam, e2_ref[h, :, cs], jnp.zeros_like(w1)) * w1
                coef_ref[ks, cs] = act[ks, cs].astype(BF16) * g
        acc_ref[...] += _dot(vt_ref[:, es], coef_ref[es, :])

    @pl.when(e == pl.num_programs(1) - 1)
    def _():
        o_ref[...] = x2_ref[...] + acc_ref[...].T


def _peer(h2_t, u, vt, r2, e2, lam, w1, x2, tm=512, te=1024, sub=128, esub=256):
    t = h2_t.shape[1]
    n_exp = u.shape[0]
    rt = pl.BlockSpec((PEER_HEADS, N_KEYS, tm), lambda i, j: (0, 0, i))
    return pl.pallas_call(
        functools.partial(_peer_body, sub=sub, esub=esub),
        grid=(t // tm, n_exp // te),
        in_specs=[pl.BlockSpec((D_MODEL, tm), lambda i, j: (0, i)),
                  pl.BlockSpec((te, D_MODEL), lambda i, j: (j, 0)),
                  pl.BlockSpec((D_MODEL, te), lambda i, j: (0, j)),
                  rt, rt, rt, rt,
                  pl.BlockSpec((tm, D_MODEL), lambda i, j: (i, 0))],
        out_specs=pl.BlockSpec((tm, D_MODEL), lambda i, j: (i, 0)),
        out_shape=jax.ShapeDtypeStruct((t, D_MODEL), F32),
        scratch_shapes=[pltpu.VMEM((D_MODEL, tm), F32), pltpu.VMEM((te, tm), BF16)],
        compiler_params=_cparams(("parallel", "arbitrary")),
        name="peer",
    )(h2_t, u, vt, r2, e2, lam, w1, x2)


def _rope_tables(pos):
    half = NSA_DH // 2
    inv = jnp.exp(_real_const(-math.log(ROPE_THETA) * 2.0) * jnp.arange(half, dtype=F32) / NSA_DH)
    ang = pos.astype(F32)[:, None] * inv[None, :]
    cos, sin = jnp.cos(ang), jnp.sin(ang)
    return jnp.concatenate([cos, cos], axis=-1), jnp.concatenate([-sin, sin], axis=-1)


def _swap_halves(a):
    half = a.shape[-1] // 2
    return jnp.concatenate([a[..., half:], a[..., :half]], axis=-1)


def _importance_matrix(n_cmp_pad, n_blk):
    r = SEL_BLK // CMP_STRIDE
    l = CMP_BLK // CMP_STRIDE
    c = jnp.arange(n_cmp_pad)[:, None]
    j = jnp.arange(n_blk)[None, :]
    o = c - r * j
    wgt = jnp.minimum(o + l, r) - jnp.maximum(o, 0)
    return jnp.where((o >= -(l - 1)) & (o < r), wgt, 0).astype(F32)


def _layer(x, attn_norm, w_in, q_gain, k_gain, cmp_pos, cmp_w1, cmp_w2, conv_w, a_log, dt_bias, out_gain,
           w_out, ffn_norm, w_query, subkeys, u_tab, v_tab):
    b, s, d = x.shape
    t = b * s
    x2d = x.reshape(t, d)
    nw = NSA_HEADS * NSA_DH
    nkv = NSA_GROUPS * NSA_DH
    gw = GDN_HEADS * GDN_DH

    o_q, o_kv, o_gate = 0, nw, nw + 6 * nkv
    o_qkvb = o_gate + 3 * NSA_HEADS
    o_z = o_qkvb + 3 * gw
    o_a = o_z + gw
    o_b = o_a + GDN_HEADS
    n_small = 3 * NSA_HEADS + 2 * GDN_HEADS
    w_perm = jnp.concatenate([w_in[:, o_q:o_gate], w_in[:, o_qkvb:o_a], w_in[:, o_gate:o_qkvb], w_in[:, o_a:],
                              jnp.zeros((d, LANES - n_small), w_in.dtype)], axis=1).astype(BF16)
    seg = jnp.kron(jnp.eye(LANES // NSA_DH, dtype=F32), jnp.full((NSA_DH, NSA_DH), 1.0 / NSA_DH, F32))
    cos, sin = _rope_tables(jnp.arange(s))
    cos2, sin2 = jnp.tile(cos, (1, 2)), jnp.tile(sin, (1, 2))
    qg = jnp.tile(q_gain, 2).reshape(1, LANES)
    kg = jnp.tile(k_gain, (1, 2))
    q_a, kv_a, qkv_b, z_b, small = _proj(x2d, attn_norm.reshape(1, d), w_perm, seg, cos2, sin2, qg, kg, s)

    def heads(a):
        return a.reshape(b, s, NSA_GROUPS, NSA_DH).transpose(0, 2, 1, 3)

    kv = [heads(kv_a[:, i * nkv:(i + 1) * nkv]) for i in range(6)]
    n16 = s // CMP_STRIDE
    a16k = kv[0].reshape(b, NSA_GROUPS, n16, CMP_STRIDE * NSA_DH)
    a16v = kv[1].reshape(b, NSA_GROUPS, n16, CMP_STRIDE * NSA_DH)
    cmp_end = jnp.arange(n16) * CMP_STRIDE + CMP_BLK - 1
    ccos, csin = _rope_tables(cmp_end)
    pos8 = jnp.broadcast_to(cmp_pos.reshape(2, 1, CMP_BLK * NSA_DH), (2, 8, CMP_BLK * NSA_DH)).astype(BF16)
    gelu_consts = jnp.stack([_real_const(math.sqrt(2.0 / math.pi)), _real_const(0.044715)]).reshape(1, 2)
    kc, vc = _compress(a16k, a16v, cmp_w1.astype(BF16), cmp_w2.astype(BF16), _swap_halves(cmp_w2[0]).astype(BF16),
                       pos8, k_gain[0:1], _swap_halves(k_gain[0:1]), ccos, csin, gelu_consts)
    n_qb = s // Q_BLK
    n_blk = s // SEL_BLK
    nq = NSA_HPG * Q_BLK
    q_t = q_a.reshape(b, n_qb, Q_BLK, NSA_GROUPS, NSA_HPG, NSA_DH).transpose(0, 3, 1, 5, 4, 2)
    q_t = q_t.reshape(b, NSA_GROUPS, n_qb, NSA_DH, nq)

    def tiles_t(a, tile):
        return a.astype(BF16).reshape(b, NSA_GROUPS, s // tile, tile, NSA_DH).transpose(0, 1, 2, 4, 3)
    gate = small[:, 0:3 * NSA_HEADS].reshape(b, n_qb, Q_BLK, NSA_GROUPS, NSA_HPG, 3)
    gate = gate.transpose(0, 3, 1, 5, 4, 2).reshape(b, NSA_GROUPS, n_qb, 3, nq)
    wimp_t = _importance_matrix(n16, n_blk).T
    own_blk = jnp.arange(s)[:, None] // SEL_BLK == jnp.arange(n_blk)[None, :]
    off = jnp.broadcast_to(jnp.where(own_blk, -MASK_OFF, 0.0).astype(BF16), (b, NSA_GROUPS, s, n_blk))
    ke = jnp.concatenate([off, kv[2].astype(BF16)], axis=-1)
    o_t = _nsa(q_t, kc, vc.transpose(0, 1, 3, 2), ke, tiles_t(kv[3], KEY_TILE), kv[4].astype(BF16),
               tiles_t(kv[5], Q_BLK), gate, wimp_t)
    o_nsa = o_t.reshape(b, NSA_GROUPS, n_qb, NSA_DH, NSA_HPG, Q_BLK).transpose(0, 2, 5, 1, 4, 3).reshape(t, nw)

    n_chunk = s // GDN_CHUNK
    eps_row = jnp.broadcast_to(_real_const(EPS), (1, GDN_DH))
    q_scale = jnp.concatenate([jnp.broadcast_to(_real_const(GDN_DH ** -0.5), (1, GDN_DH)), eps_row], axis=0)
    gq, gk, gv = _gdn_prep(qkv_b.reshape(b, s, 3 * gw), conv_w, q_scale)
    kt5 = gk.reshape(b, n_chunk, GDN_CHUNK, GDN_HEADS, GDN_DH).transpose(0, 3, 1, 4, 2)
    a_col = small[:, 3 * NSA_HEADS:3 * NSA_HEADS + GDN_HEADS].reshape(b, s, GDN_HEADS)
    b_col = small[:, 3 * NSA_HEADS + GDN_HEADS:n_small].reshape(b, s, GDN_HEADS)
    u, w, intra, qd, kdt, gl = _gdn_local(gq, gk, kt5, gv, a_col, b_col, a_col.transpose(0, 2, 1), a_log, dt_bias)
    gain_eps = jnp.concatenate([out_gain.reshape(1, GDN_DH), eps_row], axis=0)
    o_gdn = _gdn_scan(u, w, intra, qd, kdt, gl, z_b.reshape(b, s, gw), gain_eps).reshape(t, gw)

    w_out_b = w_out.astype(BF16)
    x2, h2 = _outproj(o_nsa, o_gdn, x2d, w_out_b[:nw], w_out_b[nw:], ffn_norm.reshape(1, d))

    h2_t = h2.T
    r2, e2, lam, w1 = _route(h2_t, w_query.T.astype(BF16), subkeys)
    out = _peer(h2_t, u_tab.astype(BF16), v_tab.T.astype(BF16), r2, e2, lam, w1, x2)
    return out.reshape(b, s, d)


def kernel(x, attn_norm, w_in, nsa_q_gain, nsa_k_gain, cmp_pos, cmp_w1, cmp_w2, gdn_conv, gdn_a_log, gdn_dt_bias,
           gdn_out_gain, w_out, ffn_norm, peer_w_query, peer_subkeys, peer_u, peer_v):
    for l in range(attn_norm.shape[0]):
        x = _layer(x, attn_norm[l], w_in[l], nsa_q_gain[l], nsa_k_gain[l], cmp_pos[l], cmp_w1[l], cmp_w2[l],
                   gdn_conv[l], gdn_a_log[l], gdn_dt_bias[l], gdn_out_gain[l], w_out[l], ffn_norm[l],
                   peer_w_query[l], peer_subkeys[l], peer_u[l], peer_v[l])
    return x
```

```python
import functools
import math

import jax
import jax.numpy as jnp
import numpy as np
from jax import lax
from jax.experimental import pallas as pl
from jax.experimental.pallas import tpu as pltpu

F32 = jnp.float32
BF16 = jnp.bfloat16
HI = lax.Precision.HIGHEST

D_MODEL = 1024
NSA_HEADS = 8
NSA_GROUPS = 2
NSA_HPG = NSA_HEADS // NSA_GROUPS
NSA_DH = 64
CMP_BLK = 32
CMP_STRIDE = 16
CMP_HID = 256
SEL_BLK = 64
N_SEL = 16
WINDOW = 512
Q_BLK = 128
ROPE_THETA = 10000.0
GDN_HEADS = 4
GDN_DH = 128
CONV_K = 4
GDN_CHUNK = 64
PEER_HEADS = 8
PEER_TOPK = 16
N_KEYS = 128
PEER_DKEY = 256
NEG = -1e30
MASK_OFF = 1e30
FORCE = 1e4
EPS = 1e-6

KEY_TILE = 1024
LANES = 128
VMEM_LIMIT = 56 * 1024 * 1024


def _cparams(sem):
    return pltpu.CompilerParams(dimension_semantics=sem, vmem_limit_bytes=VMEM_LIMIT)


def _dot(a, b, precision=None):
    return jnp.dot(a, b, preferred_element_type=F32, precision=precision)


def _split(a):
    hi = a.astype(BF16)
    return hi, (a - hi.astype(F32)).astype(BF16)


def _dot3(a, b):
    (a_hi, a_lo), (b_hi, b_lo) = a, b
    return _dot(a_hi, b_hi) + (_dot(a_hi, b_lo) + _dot(a_lo, b_hi))


def _dot_nt(a, b, precision=None):
    return lax.dot_general(a, b, (((1,), (1,)), ((), ())), preferred_element_type=F32, precision=precision)


def _dot_tn(a, b):
    return lax.dot_general(a, b, (((0,), (0,)), ((), ())), preferred_element_type=F32)


def _iota(shape, axis):
    return lax.broadcasted_iota(jnp.int32, shape, axis)


def _real_const(c):
    hi = np.float32(c)
    lo = np.float32(c - float(hi))
    return jnp.asarray(hi) + jnp.asarray(lo)


def _gelu_tanh(x, c0, c1):
    return x * (0.5 * (1.0 + jnp.tanh(c0 * (x + c1 * (x * x * x)))))


def _pair_norm_rope(x, seg, gain, cos, sinsg):
    sq = x * x
    p1 = sq.astype(BF16)
    r1 = sq - p1.astype(F32)
    p2 = r1.astype(BF16)
    p3 = (r1 - p2.astype(F32)).astype(BF16)
    ms = _dot(p1, seg) + (_dot(p2, seg) + _dot(p3, seg))
    y = x * lax.rsqrt(ms + EPS) * gain
    lane = _iota(y.shape, 1) % NSA_DH
    swapped = jnp.where(lane < NSA_DH // 2, pltpu.roll(y, LANES - NSA_DH // 2, 1), pltpu.roll(y, NSA_DH // 2, 1))
    return y * cos + swapped * sinsg


def _proj_body(x_ref, g_ref, w_ref, seg_ref, cos_ref, sin_ref, qg_ref, kg_ref,
               q_ref, kv_ref, qkvb_ref, zb_ref, sm_ref):
    x = x_ref[...]
    r = lax.rsqrt(jnp.mean(x * x, axis=-1, keepdims=True) + EPS)
    h = (x * r * g_ref[...]).astype(BF16)
    seg = seg_ref[...]
    cos = cos_ref[...]
    sin = sin_ref[...]
    scale = NSA_DH ** -0.5
    for j in range(4):
        p = _dot(h, w_ref[:, j * LANES:(j + 1) * LANES])
        q_ref[:, j * LANES:(j + 1) * LANES] = (_pair_norm_rope(p, seg, qg_ref[...], cos, sin) * scale).astype(BF16)
    for j in range(6):
        p = _dot(h, w_ref[:, 512 + j * LANES:512 + (j + 1) * LANES])
        if j == 2:
            p = _pair_norm_rope(p, seg, kg_ref[1:2, :], cos, sin)
        elif j == 4:
            p = _pair_norm_rope(p, seg, kg_ref[2:3, :], cos, sin)
        kv_ref[:, j * LANES:(j + 1) * LANES] = p
    qkvb_ref[...] = _dot(h, w_ref[:, 1280:2816])
    zb_ref[...] = _dot(h, w_ref[:, 2816:3328])
    sm_ref[...] = _dot(h, w_ref[:, 3328:3456])


def _proj(x2d, g, w, seg, cos, sin, qg, kg, seq, tm=256):
    t = x2d.shape[0]
    nps = seq // tm
    full = lambda shape: pl.BlockSpec(shape, lambda i: (0,) * len(shape))
    row = lambda n: pl.BlockSpec((tm, n), lambda i: (i, 0))
    return pl.pallas_call(
        _proj_body,
        grid=(t // tm,),
        in_specs=[row(D_MODEL), full((1, D_MODEL)), full(w.shape), full((LANES, LANES)),
                  pl.BlockSpec((tm, LANES), lambda i: (i % nps, 0)),
                  pl.BlockSpec((tm, LANES), lambda i: (i % nps, 0)),
                  full((1, LANES)), full((3, LANES))],
        out_specs=[row(512), row(768), row(1536), row(512), row(LANES)],
        out_shape=[jax.ShapeDtypeStruct((t, 512), BF16), jax.ShapeDtypeStruct((t, 768), F32),
                   jax.ShapeDtypeStruct((t, 1536), F32), jax.ShapeDtypeStruct((t, 512), F32),
                   jax.ShapeDtypeStruct((t, LANES), F32)],
        compiler_params=_cparams(("parallel",)),
        name="proj",
    )(x2d, g, w, seg, cos, sin, qg, kg)


def _compress_body(ak_ref, av_ref, w1_ref, w2_ref, w2p_ref, pos_ref, gain_ref, gainp_ref, cos_ref, sin_ref, gc_ref,
                   kc_ref, vc_ref):
    n = ak_ref.shape[0]
    half = ak_ref.shape[1]

    def hidden(a_ref, j):
        a = a_ref[...].astype(BF16)
        ha = _dot(a, w1_ref[j, 0:half, :])
        hb = _dot(a, w1_ref[j, half:2 * half, :])
        pb = _dot(pos_ref[j], w1_ref[j])
        hid = ha + pltpu.roll(hb, n - 1, 0) + pb[0:1, :]
        return _gelu_tanh(hid, gc_ref[0:1, 0:1], gc_ref[0:1, 1:2]).astype(BF16)

    hk = hidden(ak_ref, 0)
    ck = _dot(hk, w2_ref[0])
    ckp = _dot(hk, w2p_ref[...])
    r = lax.rsqrt(jnp.mean(ck * ck, axis=-1, keepdims=True) + EPS)
    kc = ck * r * gain_ref[...] * cos_ref[...] + ckp * r * gainp_ref[...] * sin_ref[...]
    kc_ref[...] = kc.astype(BF16)
    hv = hidden(av_ref, 1)
    vc_ref[...] = _dot(hv, w2_ref[1]).astype(BF16)


def _compress(a16k, a16v, w1, w2, w2p, pos8, gain, gainp, cos, sin, gelu_consts):
    b, g, n, half = a16k.shape
    full = lambda shape: pl.BlockSpec(shape, lambda i, j: (0,) * len(shape))
    per = lambda last: pl.BlockSpec((None, None, n, last), lambda i, j: (i, j, 0, 0))
    return pl.pallas_call(
        _compress_body,
        grid=(b, g),
        in_specs=[per(half), per(half), full(w1.shape), full(w2.shape), full(w2p.shape), full(pos8.shape),
                  full(gain.shape), full(gainp.shape), full(cos.shape), full(sin.shape), full(gelu_consts.shape)],
        out_specs=[per(NSA_DH), per(NSA_DH)],
        out_shape=[jax.ShapeDtypeStruct((b, g, n, NSA_DH), BF16)] * 2,
        compiler_params=_cparams(("parallel", "parallel")),
        name="compress",
    )(a16k, a16v, w1, w2, w2p, pos8, gain, gainp, cos, sin, gelu_consts)


def _col_softmax(s):
    e = jnp.exp(s - jnp.max(s, axis=0, keepdims=True))
    return e, jnp.sum(e, axis=0, keepdims=True)


def _nsa_body(q_ref, kc_ref, vc_ref, ke_ref, vs_ref, kw_ref, vw_ref, gate_ref, wimp_ref, o_ref, *, n_sel):
    n_cmp = kc_ref.shape[0]
    n_blk = wimp_ref.shape[0]
    nq = NSA_HPG * Q_BLK
    t0 = pl.program_id(2) * Q_BLK
    t_row = t0 + _iota((1, Q_BLK), 1)
    t_row4 = t0 + _iota((1, nq), 1) % Q_BLK
    q_t = q_ref[...]
    gates = jax.nn.sigmoid(gate_ref[...])
    lanes = lambda h: slice(h * Q_BLK, (h + 1) * Q_BLK)

    s_c = _dot(kc_ref[...], q_t)
    cmp_end = _iota((n_cmp, 1), 0) * CMP_STRIDE + (CMP_BLK - 1)
    bias_c = jnp.where(cmp_end <= t_row, 0.0, NEG)
    any_c = (t_row >= CMP_BLK - 1).astype(F32)
    vc_t = vc_ref[...]
    o_c = []
    p_sum = jnp.zeros((n_cmp, Q_BLK), F32)
    for h in range(NSA_HPG):
        e, l = _col_softmax(s_c[:, lanes(h)] + bias_c)
        p = e * (any_c / l)
        p_sum = p_sum + p
        o_c.append(_dot(vc_t, p.astype(BF16)))
    imp = _dot(wimp_ref[...], p_sum, HI)

    blk = _iota((n_blk, Q_BLK), 0)
    cur = t_row // SEL_BLK
    forced = (blk == 0) | (blk == cur) | (blk == cur - 1)
    causal = blk * SEL_BLK <= t_row
    score = jnp.where(causal, jnp.where(forced, FORCE, imp), NEG)
    sel = jnp.zeros((n_blk, Q_BLK), F32)
    for _ in range(n_sel):
        m = jnp.max(score, axis=0, keepdims=True)
        idx = jnp.min(jnp.where(score == m, blk, n_blk), axis=0, keepdims=True)
        hit = blk == idx
        sel = jnp.where(hit, 1.0, sel)
        score = jnp.where(hit, -jnp.inf, score)
    nsel = jnp.where(causal, 1.0 - sel, 1.0).astype(BF16)
    bmat = jnp.concatenate([jnp.concatenate([nsel] * NSA_HPG, axis=1), q_t], axis=0)

    def sel_tile(kt, carry, diagonal):
        m, l, acc = carry
        start = pl.multiple_of(kt * KEY_TILE, KEY_TILE)
        s = _dot(ke_ref[pl.ds(start, KEY_TILE), :], bmat)
        if diagonal:
            s = jnp.where(start + _iota((KEY_TILE, 1), 0) <= t_row4, s, NEG)
        m_new = jnp.maximum(m, jnp.max(s, axis=0, keepdims=True))
        alpha = jnp.exp(m - m_new)
        p = jnp.exp(s - m_new)
        l = alpha * l + jnp.sum(p, axis=0, keepdims=True)
        acc = alpha * acc + _dot(vs_ref[kt], p.astype(BF16))
        return m_new, l, acc

    init = (jnp.full((1, nq), NEG, F32), jnp.zeros((1, nq), F32), jnp.zeros((NSA_DH, nq), F32))
    n_full = t0 // KEY_TILE
    carry = lax.fori_loop(0, n_full, functools.partial(sel_tile, diagonal=False), init)
    _, l_s, acc_s = sel_tile(n_full, carry, True)
    o_s = acc_s / l_s

    wspan = WINDOW + Q_BLK
    wstart = pl.multiple_of(jnp.maximum(t0 - WINDOW, 0), Q_BLK)
    s_w = _dot(kw_ref[pl.ds(wstart, wspan), :], q_t)
    w0 = wstart // Q_BLK
    vwin_t = jnp.concatenate([vw_ref[w0 + i] for i in range(wspan // Q_BLK)], axis=1)
    rel = t_row - (wstart + _iota((wspan, 1), 0))
    bias_w = jnp.where((rel >= 0) & (rel < WINDOW), 0.0, NEG)

    for h in range(NSA_HPG):
        e, l = _col_softmax(s_w[:, lanes(h)] + bias_w)
        o_w = _dot(vwin_t, e.astype(BF16)) / l
        g = gates[:, lanes(h)]
        o = o_c[h] * g[0:1] + o_s[:, lanes(h)] * g[1:2] + o_w * g[2:3]
        o_ref[:, lanes(h)] = o.astype(o_ref.dtype)


def _nsa(q_t, kc, vc_t, ke, vs_t, kw, vw_t, gate, wimp):
    b, _, n_qb, _, nq = q_t.shape
    s = n_qb * Q_BLK
    n_cmp = kc.shape[2]
    n_blk = s // SEL_BLK
    per_g = lambda a: pl.BlockSpec((None, None) + a.shape[2:], lambda i, j, k: (i, j) + (0,) * (a.ndim - 2))
    per_q = lambda rows: pl.BlockSpec((None, None, None, rows, nq), lambda i, j, k: (i, j, k, 0, 0))
    return pl.pallas_call(
        functools.partial(_nsa_body, n_sel=min(N_SEL, n_blk)),
        grid=(b, NSA_GROUPS, n_qb),
        in_specs=[per_q(NSA_DH), per_g(kc), per_g(vc_t), per_g(ke), per_g(vs_t), per_g(kw), per_g(vw_t), per_q(3),
                  pl.BlockSpec(wimp.shape, lambda i, j, k: (0, 0))],
        out_specs=per_q(NSA_DH),
        out_shape=jax.ShapeDtypeStruct((b, NSA_GROUPS, n_qb, NSA_DH, nq), BF16),
        compiler_params=_cparams(("parallel", "parallel", "arbitrary")),
        name="nsa",
    )(q_t, kc, vc_t, ke, vs_t, kw, vw_t, gate, wimp)


def _gdn_prep_body(x_ref, prev_ref, w_ref, qs_ref, q_ref, k_ref, v_ref, pad_ref):
    tm = x_ref.shape[0]
    width = GDN_HEADS * GDN_DH
    first = pl.program_id(1) == 0
    pad_ref[0:8, :] = jnp.where(first, 0.0, prev_ref[...])
    pad_ref[8:8 + tm, :] = x_ref[...]
    conv = pad_ref[8:8 + tm, :] * w_ref[CONV_K - 1:CONV_K, :]
    for j in range(CONV_K - 1):
        sh = CONV_K - 1 - j
        conv = conv + pad_ref[8 - sh:8 - sh + tm, :] * w_ref[j:j + 1, :]
    conv = conv * jax.nn.sigmoid(conv)
    for part, ref in enumerate((q_ref, k_ref, v_ref)):
        for h in range(GDN_HEADS):
            lo = part * width + h * GDN_DH
            c = conv[:, lo:lo + GDN_DH]
            if part < 2:
                c = c * lax.rsqrt(jnp.sum(c * c, axis=-1, keepdims=True) + qs_ref[1:2, 0:1])
            if part == 0:
                c = c * qs_ref[0:1, :]
            ref[:, h * GDN_DH:(h + 1) * GDN_DH] = c


def _gdn_prep(qkv, conv_w, q_scale, tm=256):
    b, s, w3 = qkv.shape
    width = w3 // 3
    out = pl.BlockSpec((None, tm, width), lambda i, j: (i, j, 0))
    return pl.pallas_call(
        _gdn_prep_body,
        grid=(b, s // tm),
        in_specs=[pl.BlockSpec((None, tm, w3), lambda i, j: (i, j, 0)),
                  pl.BlockSpec((None, 8, w3), lambda i, j: (i, jnp.maximum(j * (tm // 8) - 1, 0), 0)),
                  pl.BlockSpec(conv_w.shape, lambda i, j: (0, 0)),
                  pl.BlockSpec(q_scale.shape, lambda i, j: (0, 0))],
        out_specs=[out, out, out],
        out_shape=[jax.ShapeDtypeStruct((b, s, width), F32)] * 3,
        scratch_shapes=[pltpu.VMEM((tm + 8, w3), F32)],
        compiler_params=_cparams(("parallel", "parallel")),
        name="gdn_prep",
    )(qkv, qkv, conv_w, q_scale)


def _gdn_local_body(q_ref, k_ref, kt_ref, v_ref, ac_ref, bc_ref, ar_ref, alog_c_ref, dtb_c_ref, alog_r_ref, dtb_r_ref,
                    u_ref, w_ref, intra_ref, qd_ref, kdt_ref, gl_ref, *, nc):
    c = GDN_CHUNK
    row = _iota((c, c), 0)
    col = _iota((c, c), 1)
    incl = row >= col
    strict = row > col
    tri = incl.astype(F32)
    eye = (row == col).astype(F32)
    xs, ts, rhs_u, rhs_w, where_to = [], [], [], [], []
    for ci in range(nc):
        rs = slice(ci * c, (ci + 1) * c)
        g_c = -jnp.exp(alog_c_ref[...]) * jax.nn.softplus(ac_ref[rs, :] + dtb_c_ref[...])
        g_r = -jnp.exp(alog_r_ref[...]) * jax.nn.softplus(ar_ref[:, rs] + dtb_r_ref[...])
        gc_all = _dot(tri, g_c, HI)
        gr_all = _dot_nt(g_r, tri, HI)
        beta_all = jax.nn.sigmoid(bc_ref[rs, :])
        gl_ref[ci] = jnp.exp(gc_all[c - 1:c, :])
        for h in range(GDN_HEADS):
            hs = slice(h * GDN_DH, (h + 1) * GDN_DH)
            gc = gc_all[:, h:h + 1]
            gr = gr_all[h:h + 1, :]
            g_last = gc_all[c - 1:c, h:h + 1]
            beta = beta_all[:, h:h + 1]
            q = q_ref[rs, hs]
            kt = kt_ref[h, ci]
            kb = k_ref[rs, hs] * beta
            decay = jnp.where(incl, jnp.exp(jnp.where(incl, gc - gr, 0.0)), 0.0)
            a = jnp.where(strict, _dot(kb, kt) * decay, 0.0)
            eg = jnp.exp(gc)
            xs.append(_split(-a))
            ts.append(eye - a)
            rhs_u.append(v_ref[rs, hs] * beta)
            rhs_w.append(kb * eg)
            where_to.append((rs, hs))
            intra_ref[h, ci] = jnp.where(incl, _dot(q, kt) * decay, 0.0)
            qd_ref[rs, hs] = q * eg
            kdt_ref[h, ci] = kt * jnp.exp(g_last - gr)
    for _ in range(int(math.log2(c)) - 1):
        xs = [_dot3(x, x) for x in xs]
        xs = [_split(x) for x in xs]
        ts = [t + _dot3(_split(t), x) for t, x in zip(ts, xs)]
    for t, ru, rw, (rs, hs) in zip(ts, rhs_u, rhs_w, where_to):
        t = _split(t)
        u_ref[rs, hs] = _dot3(t, _split(ru))
        w_ref[rs, hs] = _dot3(t, _split(rw))


def _gdn_local(q, k, kt5, v, a_col, b_col, a_row, alog, dtb, nc=2):
    b, s, width = q.shape
    n = s // GDN_CHUNK
    hh = GDN_HEADS
    rows = nc * GDN_CHUNK
    tok = pl.BlockSpec((None, rows, width), lambda i, j: (i, j, 0))
    per5 = lambda d0, d1: pl.BlockSpec((None, hh, nc, d0, d1), lambda i, j: (i, 0, j, 0, 0))
    small = lambda shape: pl.BlockSpec(shape, lambda i, j: (0,) * len(shape))
    return pl.pallas_call(
        functools.partial(_gdn_local_body, nc=nc),
        grid=(b, n // nc),
        in_specs=[tok, tok, per5(GDN_DH, GDN_CHUNK), tok,
                  pl.BlockSpec((None, rows, hh), lambda i, j: (i, j, 0)),
                  pl.BlockSpec((None, rows, hh), lambda i, j: (i, j, 0)),
                  pl.BlockSpec((None, hh, rows), lambda i, j: (i, 0, j)),
                  small((1, hh)), small((1, hh)), small((hh, 1)), small((hh, 1))],
        out_specs=[tok, tok, per5(GDN_CHUNK, GDN_CHUNK), tok, per5(GDN_DH, GDN_CHUNK),
                   pl.BlockSpec((None, nc, 1, hh), lambda i, j: (i, j, 0, 0))],
        out_shape=[jax.ShapeDtypeStruct((b, s, width), F32), jax.ShapeDtypeStruct((b, s, width), F32),
                   jax.ShapeDtypeStruct((b, hh, n, GDN_CHUNK, GDN_CHUNK), F32),
                   jax.ShapeDtypeStruct((b, s, width), F32),
                   jax.ShapeDtypeStruct((b, hh, n, GDN_DH, GDN_CHUNK), F32),
                   jax.ShapeDtypeStruct((b, n, 1, hh), F32)],
        compiler_params=_cparams(("parallel", "parallel")),
        name="gdn_local",
    )(q, k, kt5, v, a_col, b_col, a_row, alog.reshape(1, hh), dtb.reshape(1, hh), alog.reshape(hh, 1),
      dtb.reshape(hh, 1))


def _gdn_scan_body(u_ref, w_ref, intra_ref, qd_ref, kdt_ref, gl_ref, z_ref, gain_ref, o_ref, state_ref, *, nc):
    c = GDN_CHUNK

    @pl.when(pl.program_id(1) == 0)
    def _():
        state_ref[...] = jnp.zeros_like(state_ref)

    heads = range(GDN_HEADS)
    hs = [slice(h * GDN_DH, (h + 1) * GDN_DH) for h in heads]
    for ci in range(nc):
        rs = slice(ci * c, (ci + 1) * c)
        gl = gl_ref[ci]
        st = [state_ref[h] for h in heads]
        st2 = [_split(s) for s in st]
        v_new = [u_ref[rs, hs[h]] - _dot3(_split(w_ref[rs, hs[h]]), st2[h]) for h in heads]
        o_st = [_dot3(_split(qd_ref[rs, hs[h]]), st2[h]) for h in heads]
        vn2 = [_split(v) for v in v_new]
        for h in heads:
            state_ref[h] = st[h] * gl[:, h:h + 1] + _dot3(_split(kdt_ref[h, ci]), vn2[h])
        for h in heads:
            o = o_st[h] + _dot3(_split(intra_ref[h, ci]), vn2[h])
            y = o * lax.rsqrt(jnp.mean(o * o, axis=-1, keepdims=True) + gain_ref[1:2, 0:1]) * gain_ref[0:1, :]
            z = z_ref[rs, hs[h]]
            o_ref[rs, hs[h]] = (y * (z * jax.nn.sigmoid(z))).astype(o_ref.dtype)


def _gdn_scan(u, w, intra, qd, kdt, gl, z, gain, nc=2):
    b, s, width = u.shape
    n = s // GDN_CHUNK
    hh = GDN_HEADS
    rows = nc * GDN_CHUNK
    tok = pl.BlockSpec((None, rows, width), lambda i, j: (i, j, 0))
    per5 = lambda d0, d1: pl.BlockSpec((None, hh, nc, d0, d1), lambda i, j: (i, 0, j, 0, 0))
    return pl.pallas_call(
        functools.partial(_gdn_scan_body, nc=nc),
        grid=(b, n // nc),
        in_specs=[tok, tok, per5(GDN_CHUNK, GDN_CHUNK), tok, per5(GDN_DH, GDN_CHUNK),
                  pl.BlockSpec((None, nc, 1, hh), lambda i, j: (i, j, 0, 0)), tok,
                  pl.BlockSpec(gain.shape, lambda i, j: (0, 0))],
        out_specs=tok,
        out_shape=jax.ShapeDtypeStruct((b, s, width), BF16),
        scratch_shapes=[pltpu.VMEM((hh, GDN_DH, GDN_DH), F32)],
        compiler_params=_cparams(("parallel", "arbitrary")),
        name="gdn_scan",
    )(u, w, intra, qd, kdt, gl, z, gain)


def _outproj_body(oa_ref, ob_ref, x_ref, wa_ref, wb_ref, g_ref, x2_ref, h2_ref):
    x2 = x_ref[...] + _dot(oa_ref[...], wa_ref[...]) + _dot(ob_ref[...], wb_ref[...])
    x2_ref[...] = x2
    r = lax.rsqrt(jnp.mean(x2 * x2, axis=-1, keepdims=True) + EPS)
    h2_ref[...] = (x2 * r * g_ref[...]).T.astype(BF16)


def _outproj(oa, ob, x2d, wa, wb, g, tm=512):
    t = x2d.shape[0]
    row = lambda n: pl.BlockSpec((tm, n), lambda i: (i, 0))
    full = lambda shape: pl.BlockSpec(shape, lambda i: (0,) * len(shape))
    return pl.pallas_call(
        _outproj_body,
        grid=(t // tm,),
        in_specs=[row(oa.shape[1]), row(ob.shape[1]), row(D_MODEL), full(wa.shape), full(wb.shape), full((1, D_MODEL))],
        out_specs=[row(D_MODEL), pl.BlockSpec((D_MODEL, tm), lambda i: (0, i))],
        out_shape=[jax.ShapeDtypeStruct((t, D_MODEL), F32), jax.ShapeDtypeStruct((D_MODEL, t), BF16)],
        compiler_params=_cparams(("parallel",)),
        name="outproj",
    )(oa, ob, x2d, wa, wb, g)


def _top_ranks(s, k, want_rank):
    n = s.shape[0]
    rows = _iota(s.shape, 0)
    rank = jnp.full(s.shape, float(k), F32) if want_rank else None
    vals, idxs = [], []
    for r in range(k):
        m = jnp.max(s, axis=0, keepdims=True)
        idx = jnp.min(jnp.where(s == m, rows, n), axis=0, keepdims=True)
        hit = rows == idx
        if want_rank:
            rank = jnp.where(hit, float(r), rank)
        s = jnp.where(hit, -jnp.inf, s)
        vals.append(m)
        idxs.append(idx)
    return rank, vals, idxs


def _route_body(h_ref, wq_ref, sk_ref, r2_ref, e2_ref, lam_ref, w1_ref):
    tm = h_ref.shape[1]
    kk = PEER_TOPK
    half = PEER_DKEY // 2
    qt = _dot(wq_ref[...], h_ref[...])
    ridx = _iota((80, tm), 0)
    grp = ridx // 8
    sub = ridx % 8
    pos = jnp.where(grp <= 1, ridx, jnp.where(grp <= 8, (grp - 1) * kk + sub, (8 + sub) * kk))
    for h in range(PEER_HEADS):
        s1 = _dot(sk_ref[0], qt[h * PEER_DKEY:h * PEER_DKEY + half], HI)
        s2 = _dot(sk_ref[1], qt[h * PEER_DKEY + half:(h + 1) * PEER_DKEY], HI)
        _, v1, idx1 = _top_ranks(s1, kk, False)
        rank2, v2, _ = _top_ranks(s2, kk, True)
        v2lo = jnp.concatenate(v2[0:8], axis=0)
        v2hi = jnp.concatenate(v2[8:16], axis=0)
        v1hi = jnp.concatenate(v1[8:16], axis=0)
        cand = jnp.concatenate([v1[0] + v2lo, v1[0] + v2hi] + [v1[r] + v2lo for r in range(1, 8)] + [v1hi + v2[0]],
                               axis=0)
        cmax = v1[0] + v2[0]
        z = jnp.zeros((1, tm), F32)
        chosen = jnp.zeros((80, tm), F32)
        for _ in range(kk):
            m = jnp.max(cand, axis=0, keepdims=True)
            p = jnp.min(jnp.where(cand == m, pos, 4 * kk * kk), axis=0, keepdims=True)
            hit = pos == p
            chosen = jnp.where(hit, 1.0, chosen)
            cand = jnp.where(hit, -jnp.inf, cand)
            z = z + jnp.exp(m - cmax)
        cnt = [jnp.sum(chosen[0:16], axis=0, keepdims=True)]
        cnt += [jnp.sum(chosen[8 * (r + 1):8 * (r + 2)], axis=0, keepdims=True) for r in range(1, 8)]
        cnt += [chosen[72 + r:73 + r] for r in range(8)]
        lam = jnp.zeros((N_KEYS, tm), F32)
        keys = _iota((N_KEYS, tm), 0)
        for r in range(kk):
            lam = jnp.where(keys == idx1[r], cnt[r], lam)
        r2_ref[h] = rank2.astype(BF16)
        e2_ref[h] = jnp.exp(s2 - v2[0]).astype(BF16)
        lam_ref[h] = lam
        w1_ref[h] = jnp.exp(s1 - v1[0]) / z


def _route(h2_t, wq_t, subkeys, tm=256):
    t = h2_t.shape[1]
    full = lambda shape: pl.BlockSpec(shape, lambda i: (0,) * len(shape))
    out = pl.BlockSpec((PEER_HEADS, N_KEYS, tm), lambda i: (0, 0, i))
    return pl.pallas_call(
        _route_body,
        grid=(t // tm,),
        in_specs=[pl.BlockSpec((D_MODEL, tm), lambda i: (0, i)), full(wq_t.shape), full(subkeys.shape)],
        out_specs=[out] * 4,
        out_shape=[jax.ShapeDtypeStruct((PEER_HEADS, N_KEYS, t), dt) for dt in (BF16, BF16, F32, F32)],
        compiler_params=_cparams(("parallel",)),
        name="route",
    )(h2_t, wq_t, subkeys)


def _peer_body(h_ref, u_ref, vt_ref, r2_ref, e2_ref, lam_ref, w1_ref, x2_ref, o_ref, acc_ref, coef_ref, *, sub,
               esub):
    te = u_ref.shape[0]
    tm = h_ref.shape[1]
    e = pl.program_id(1)

    @pl.when(e == 0)
    def _():
        acc_ref[...] = jnp.zeros_like(acc_ref)

    n1 = te // N_KEYS
    i1_lo = pl.multiple_of(e * n1, n1)
    hq = h_ref[...]
    act = jax.nn.gelu(_dot(u_ref[...], hq))
    per = esub // N_KEYS
    for s in range(te // esub):
        es = slice(s * esub, (s + 1) * esub)
        for kk in range(per):
            k = s * per + kk
            ks = slice(k * N_KEYS, (k + 1) * N_KEYS)
            for c in range(tm // sub):
                cs = slice(c * sub, (c + 1) * sub)
                g = jnp.zeros((N_KEYS, sub), BF16)
                for h in range(PEER_HEADS):
                    lam = jnp.broadcast_to(lam_ref[h, pl.ds(i1_lo, n1), cs][k:k + 1].astype(BF16), (N_KEYS, sub))
                    w1 = jnp.broadcast_to(w1_ref[h, pl.ds(i1_lo, n1), cs][k:k + 1].astype(BF16), (N_KEYS, sub))
                    g = g + jnp.where(r2_ref[h, :, cs] < lam, e2_ref[h, :, cs], jnp.zeros_like(w1)) * w1
                coef_ref[ks, cs] = act[ks, cs].astype(BF16) * g
        acc_ref[...] += _dot(vt_ref[:, es], coef_ref[es, :])

    @pl.when(e == pl.num_programs(1) - 1)
    def _():
        o_ref[...] = x2_ref[...] + acc_ref[...].T


def _peer(h2_t, u, vt, r2, e2, lam, w1, x2, tm=512, te=2048, sub=128, esub=256):
    t = h2_t.shape[1]
    n_exp = u.shape[0]
    rt = pl.BlockSpec((PEER_HEADS, N_KEYS, tm), lambda i, j: (0, 0, i))
    return pl.pallas_call(
        functools.partial(_peer_body, sub=sub, esub=esub),
        grid=(t // tm, n_exp // te),
        in_specs=[pl.BlockSpec((D_MODEL, tm), lambda i, j: (0, i)),
                  pl.BlockSpec((te, D_MODEL), lambda i, j: (j, 0)),
                  pl.BlockSpec((D_MODEL, te), lambda i, j: (0, j)),
                  rt, rt, rt, rt,
                  pl.BlockSpec((tm, D_MODEL), lambda i, j: (i, 0))],
        out_specs=pl.BlockSpec((tm, D_MODEL), lambda i, j: (i, 0)),
        out_shape=jax.ShapeDtypeStruct((t, D_MODEL), F32),
        scratch_shapes=[pltpu.VMEM((D_MODEL, tm), F32), pltpu.VMEM((te, tm), BF16)],
        compiler_params=_cparams(("parallel", "arbitrary")),
        name="peer",
    )(h2_t, u, vt, r2, e2, lam, w1, x2)


def _rope_tables(pos):
    half = NSA_DH // 2
    inv = jnp.exp(_real_const(-math.log(ROPE_THETA) * 2.0) * jnp.arange(half, dtype=F32) / NSA_DH)
    ang = pos.astype(F32)[:, None] * inv[None, :]
    cos, sin = jnp.cos(ang), jnp.sin(ang)
    return jnp.concatenate([cos, cos], axis=-1), jnp.concatenate([-sin, sin], axis=-1)


def _swap_halves(a):
    half = a.shape[-1] // 2
    return jnp.concatenate([a[..., half:], a[..., :half]], axis=-1)


def _importance_matrix(n_cmp_pad, n_blk):
    r = SEL_BLK // CMP_STRIDE
    l = CMP_BLK // CMP_STRIDE
    c = jnp.arange(n_cmp_pad)[:, None]
    j = jnp.arange(n_blk)[None, :]
    o = c - r * j
    wgt = jnp.minimum(o + l, r) - jnp.maximum(o, 0)
    return jnp.where((o >= -(l - 1)) & (o < r), wgt, 0).astype(F32)


def _layer(x, attn_norm, w_in, q_gain, k_gain, cmp_pos, cmp_w1, cmp_w2, conv_w, a_log, dt_bias, out_gain,
           w_out, ffn_norm, w_query, subkeys, u_tab, v_tab):
    b, s, d = x.shape
    t = b * s
    x2d = x.reshape(t, d)
    nw = NSA_HEADS * NSA_DH
    nkv = NSA_GROUPS * NSA_DH
    gw = GDN_HEADS * GDN_DH

    o_q, o_kv, o_gate = 0, nw, nw + 6 * nkv
    o_qkvb = o_gate + 3 * NSA_HEADS
    o_z = o_qkvb + 3 * gw
    o_a = o_z + gw
    o_b = o_a + GDN_HEADS
    n_small = 3 * NSA_HEADS + 2 * GDN_HEADS
    w_perm = jnp.concatenate([w_in[:, o_q:o_gate], w_in[:, o_qkvb:o_a], w_in[:, o_gate:o_qkvb], w_in[:, o_a:],
                              jnp.zeros((d, LANES - n_small), w_in.dtype)], axis=1).astype(BF16)
    seg = jnp.kron(jnp.eye(LANES // NSA_DH, dtype=F32), jnp.full((NSA_DH, NSA_DH), 1.0 / NSA_DH, F32)).astype(BF16)
    cos, sin = _rope_tables(jnp.arange(s))
    cos2, sin2 = jnp.tile(cos, (1, 2)), jnp.tile(sin, (1, 2))
    qg = jnp.tile(q_gain, 2).reshape(1, LANES)
    kg = jnp.tile(k_gain, (1, 2))
    q_a, kv_a, qkv_b, z_b, small = _proj(x2d, attn_norm.reshape(1, d), w_perm, seg, cos2, sin2, qg, kg, s)

    def heads(a):
        return a.reshape(b, s, NSA_GROUPS, NSA_DH).transpose(0, 2, 1, 3)

    kv = [heads(kv_a[:, i * nkv:(i + 1) * nkv]) for i in range(6)]
    n16 = s // CMP_STRIDE
    a16k = kv[0].reshape(b, NSA_GROUPS, n16, CMP_STRIDE * NSA_DH)
    a16v = kv[1].reshape(b, NSA_GROUPS, n16, CMP_STRIDE * NSA_DH)
    cmp_end = jnp.arange(n16) * CMP_STRIDE + CMP_BLK - 1
    ccos, csin = _rope_tables(cmp_end)
    pos8 = jnp.broadcast_to(cmp_pos.reshape(2, 1, CMP_BLK * NSA_DH), (2, 8, CMP_BLK * NSA_DH)).astype(BF16)
    gelu_consts = jnp.stack([_real_const(math.sqrt(2.0 / math.pi)), _real_const(0.044715)]).reshape(1, 2)
    kc, vc = _compress(a16k, a16v, cmp_w1.astype(BF16), cmp_w2.astype(BF16), _swap_halves(cmp_w2[0]).astype(BF16),
                       pos8, k_gain[0:1], _swap_halves(k_gain[0:1]), ccos, csin, gelu_consts)
    n_qb = s // Q_BLK
    n_blk = s // SEL_BLK
    nq = NSA_HPG * Q_BLK
    q_t = q_a.reshape(b, n_qb, Q_BLK, NSA_GROUPS, NSA_HPG, NSA_DH).transpose(0, 3, 1, 5, 4, 2)
    q_t = q_t.reshape(b, NSA_GROUPS, n_qb, NSA_DH, nq)

    def tiles_t(a, tile):
        return a.astype(BF16).reshape(b, NSA_GROUPS, s // tile, tile, NSA_DH).transpose(0, 1, 2, 4, 3)
    gate = small[:, 0:3 * NSA_HEADS].reshape(b, n_qb, Q_BLK, NSA_GROUPS, NSA_HPG, 3)
    gate = gate.transpose(0, 3, 1, 5, 4, 2).reshape(b, NSA_GROUPS, n_qb, 3, nq)
    wimp_t = _importance_matrix(n16, n_blk).T
    own_blk = jnp.arange(s)[:, None] // SEL_BLK == jnp.arange(n_blk)[None, :]
    off = jnp.broadcast_to(jnp.where(own_blk, -MASK_OFF, 0.0).astype(BF16), (b, NSA_GROUPS, s, n_blk))
    ke = jnp.concatenate([off, kv[2].astype(BF16)], axis=-1)
    o_t = _nsa(q_t, kc, vc.transpose(0, 1, 3, 2), ke, tiles_t(kv[3], KEY_TILE), kv[4].astype(BF16),
               tiles_t(kv[5], Q_BLK), gate, wimp_t)
    o_nsa = o_t.reshape(b, NSA_GROUPS, n_qb, NSA_DH, NSA_HPG, Q_BLK).transpose(0, 2, 5, 1, 4, 3).reshape(t, nw)

    n_chunk = s // GDN_CHUNK
    eps_row = jnp.broadcast_to(_real_const(EPS), (1, GDN_DH))
    q_scale = jnp.concatenate([jnp.broadcast_to(_real_const(GDN_DH ** -0.5), (1, GDN_DH)), eps_row], axis=0)
    gq, gk, gv = _gdn_prep(qkv_b.reshape(b, s, 3 * gw), conv_w, q_scale)
    kt5 = gk.reshape(b, n_chunk, GDN_CHUNK, GDN_HEADS, GDN_DH).transpose(0, 3, 1, 4, 2)
    a_col = small[:, 3 * NSA_HEADS:3 * NSA_HEADS + GDN_HEADS].reshape(b, s, GDN_HEADS)
    b_col = small[:, 3 * NSA_HEADS + GDN_HEADS:n_small].reshape(b, s, GDN_HEADS)
    u, w, intra, qd, kdt, gl = _gdn_local(gq, gk, kt5, gv, a_col, b_col, a_col.transpose(0, 2, 1), a_log, dt_bias)
    gain_eps = jnp.concatenate([out_gain.reshape(1, GDN_DH), eps_row], axis=0)
    o_gdn = _gdn_scan(u, w, intra, qd, kdt, gl, z_b.reshape(b, s, gw), gain_eps).reshape(t, gw)

    w_out_b = w_out.astype(BF16)
    x2, h2_t = _outproj(o_nsa, o_gdn, x2d, w_out_b[:nw], w_out_b[nw:], ffn_norm.reshape(1, d))

    r2, e2, lam, w1 = _route(h2_t, w_query.T.astype(BF16), subkeys)
    out = _peer(h2_t, u_tab.astype(BF16), v_tab.T.astype(BF16), r2, e2, lam, w1, x2)
    return out.reshape(b, s, d)


def kernel(x, attn_norm, w_in, nsa_q_gain, nsa_k_gain, cmp_pos, cmp_w1, cmp_w2, gdn_conv, gdn_a_log, gdn_dt_bias,
           gdn_out_gain, w_out, ffn_norm, peer_w_query, peer_subkeys, peer_u, peer_v):
    for l in range(attn_norm.shape[0]):
        x = _layer(x, attn_norm[l], w_in[l], nsa_q_gain[l], nsa_k_gain[l], cmp_pos[l], cmp_w1[l], cmp_w2[l],
                   gdn_conv[l], gdn_a_log[l], gdn_dt_bias[l], gdn_out_gain[l], w_out[l], ffn_norm[l],
                   peer_w_query[l], peer_subkeys[l], peer_u[l], peer_v[l])
    return x
```

```python
import functools
import math

import jax
import jax.numpy as jnp
import numpy as np
from jax import lax
from jax.experimental import pallas as pl
from jax.experimental.pallas import tpu as pltpu

F32 = jnp.float32
BF16 = jnp.bfloat16
HI = lax.Precision.HIGHEST

D_MODEL = 1024
NSA_HEADS = 8
NSA_GROUPS = 2
NSA_HPG = NSA_HEADS // NSA_GROUPS
NSA_DH = 64
CMP_BLK = 32
CMP_STRIDE = 16
CMP_HID = 256
SEL_BLK = 64
N_SEL = 16
WINDOW = 512
Q_BLK = 128
ROPE_THETA = 10000.0
GDN_HEADS = 4
GDN_DH = 128
CONV_K = 4
GDN_CHUNK = 64
PEER_HEADS = 8
PEER_TOPK = 16
N_KEYS = 128
PEER_DKEY = 256
NEG = -1e30
MASK_OFF = 1e30
FORCE = 1e4
EPS = 1e-6

KEY_TILE = 1024
LANES = 128
VMEM_LIMIT = 56 * 1024 * 1024


def _cparams(sem):
    return pltpu.CompilerParams(dimension_semantics=sem, vmem_limit_bytes=VMEM_LIMIT)


def _dot(a, b, precision=None):
    return jnp.dot(a, b, preferred_element_type=F32, precision=precision)


def _split(a):
    hi = a.astype(BF16)
    return hi, (a - hi.astype(F32)).astype(BF16)


def _dot3(a, b):
    (a_hi, a_lo), (b_hi, b_lo) = a, b
    return _dot(a_hi, b_hi) + (_dot(a_hi, b_lo) + _dot(a_lo, b_hi))


def _dot_nt(a, b, precision=None):
    return lax.dot_general(a, b, (((1,), (1,)), ((), ())), preferred_element_type=F32, precision=precision)


def _dot_tn(a, b):
    return lax.dot_general(a, b, (((0,), (0,)), ((), ())), preferred_element_type=F32)


def _iota(shape, axis):
    return lax.broadcasted_iota(jnp.int32, shape, axis)


def _real_const(c):
    hi = np.float32(c)
    lo = np.float32(c - float(hi))
    return jnp.asarray(hi) + jnp.asarray(lo)


def _gelu_tanh(x, c0, c1):
    return x * (0.5 * (1.0 + jnp.tanh(c0 * (x + c1 * (x * x * x)))))


def _pair_norm_rope(x, seg, gain, cos, sinsg):
    sq = x * x
    p1 = sq.astype(BF16)
    r1 = sq - p1.astype(F32)
    p2 = r1.astype(BF16)
    p3 = (r1 - p2.astype(F32)).astype(BF16)
    ms = _dot(p1, seg) + (_dot(p2, seg) + _dot(p3, seg))
    y = x * lax.rsqrt(ms + EPS) * gain
    lane = _iota(y.shape, 1) % NSA_DH
    swapped = jnp.where(lane < NSA_DH // 2, pltpu.roll(y, LANES - NSA_DH // 2, 1), pltpu.roll(y, NSA_DH // 2, 1))
    return y * cos + swapped * sinsg


def _proj_body(x_ref, g_ref, w_ref, seg_ref, cos_ref, sin_ref, qg_ref, kg_ref,
               q_ref, kv_ref, qkvb_ref, zb_ref, sm_ref):
    x = x_ref[...]
    r = lax.rsqrt(jnp.mean(x * x, axis=-1, keepdims=True) + EPS)
    h = (x * r * g_ref[...]).astype(BF16)
    seg = seg_ref[...]
    cos = cos_ref[...]
    sin = sin_ref[...]
    scale = NSA_DH ** -0.5
    for j in range(4):
        p = _dot(h, w_ref[:, j * LANES:(j + 1) * LANES])
        q_ref[:, j * LANES:(j + 1) * LANES] = (_pair_norm_rope(p, seg, qg_ref[...], cos, sin) * scale).astype(BF16)
    for j in range(6):
        p = _dot(h, w_ref[:, 512 + j * LANES:512 + (j + 1) * LANES])
        if j == 2:
            p = _pair_norm_rope(p, seg, kg_ref[1:2, :], cos, sin)
        elif j == 4:
            p = _pair_norm_rope(p, seg, kg_ref[2:3, :], cos, sin)
        kv_ref[:, j * LANES:(j + 1) * LANES] = p
    qkvb_ref[...] = _dot(h, w_ref[:, 1280:2816])
    zb_ref[...] = _dot(h, w_ref[:, 2816:3328])
    sm_ref[...] = _dot(h, w_ref[:, 3328:3456])


def _proj(x2d, g, w, seg, cos, sin, qg, kg, seq, tm=256):
    t = x2d.shape[0]
    nps = seq // tm
    full = lambda shape: pl.BlockSpec(shape, lambda i: (0,) * len(shape))
    row = lambda n: pl.BlockSpec((tm, n), lambda i: (i, 0))
    return pl.pallas_call(
        _proj_body,
        grid=(t // tm,),
        in_specs=[row(D_MODEL), full((1, D_MODEL)), full(w.shape), full((LANES, LANES)),
                  pl.BlockSpec((tm, LANES), lambda i: (i % nps, 0)),
                  pl.BlockSpec((tm, LANES), lambda i: (i % nps, 0)),
                  full((1, LANES)), full((3, LANES))],
        out_specs=[row(512), row(768), row(1536), row(512), row(LANES)],
        out_shape=[jax.ShapeDtypeStruct((t, 512), BF16), jax.ShapeDtypeStruct((t, 768), F32),
                   jax.ShapeDtypeStruct((t, 1536), F32), jax.ShapeDtypeStruct((t, 512), F32),
                   jax.ShapeDtypeStruct((t, LANES), F32)],
        compiler_params=_cparams(("parallel",)),
        name="proj",
    )(x2d, g, w, seg, cos, sin, qg, kg)


def _compress_body(ak_ref, av_ref, w1_ref, w2_ref, w2p_ref, pos_ref, gain_ref, gainp_ref, cos_ref, sin_ref, gc_ref,
                   kc_ref, vc_ref):
    n = ak_ref.shape[0]
    half = ak_ref.shape[1]

    def hidden(a_ref, j):
        a = a_ref[...].astype(BF16)
        ha = _dot(a, w1_ref[j, 0:half, :])
        hb = _dot(a, w1_ref[j, half:2 * half, :])
        pb = _dot(pos_ref[j], w1_ref[j])
        hid = ha + pltpu.roll(hb, n - 1, 0) + pb[0:1, :]
        return _gelu_tanh(hid, gc_ref[0:1, 0:1], gc_ref[0:1, 1:2]).astype(BF16)

    hk = hidden(ak_ref, 0)
    ck = _dot(hk, w2_ref[0])
    ckp = _dot(hk, w2p_ref[...])
    r = lax.rsqrt(jnp.mean(ck * ck, axis=-1, keepdims=True) + EPS)
    kc = ck * r * gain_ref[...] * cos_ref[...] + ckp * r * gainp_ref[...] * sin_ref[...]
    kc_ref[...] = kc.astype(BF16)
    hv = hidden(av_ref, 1)
    vc_ref[...] = _dot(hv, w2_ref[1]).astype(BF16)


def _compress(a16k, a16v, w1, w2, w2p, pos8, gain, gainp, cos, sin, gelu_consts):
    b, g, n, half = a16k.shape
    full = lambda shape: pl.BlockSpec(shape, lambda i, j: (0,) * len(shape))
    per = lambda last: pl.BlockSpec((None, None, n, last), lambda i, j: (i, j, 0, 0))
    return pl.pallas_call(
        _compress_body,
        grid=(b, g),
        in_specs=[per(half), per(half), full(w1.shape), full(w2.shape), full(w2p.shape), full(pos8.shape),
                  full(gain.shape), full(gainp.shape), full(cos.shape), full(sin.shape), full(gelu_consts.shape)],
        out_specs=[per(NSA_DH), per(NSA_DH)],
        out_shape=[jax.ShapeDtypeStruct((b, g, n, NSA_DH), BF16)] * 2,
        compiler_params=_cparams(("parallel", "parallel")),
        name="compress",
    )(a16k, a16v, w1, w2, w2p, pos8, gain, gainp, cos, sin, gelu_consts)


def _col_softmax(s):
    e = jnp.exp(s - jnp.max(s, axis=0, keepdims=True))
    return e, jnp.sum(e, axis=0, keepdims=True)


def _nsa_body(q_ref, kc_ref, vc_ref, ke_ref, vs_ref, kw_ref, vw_ref, gate_ref, wimp_ref, o_ref, *, n_sel):
    n_cmp = kc_ref.shape[0]
    n_blk = wimp_ref.shape[0]
    nq = NSA_HPG * Q_BLK
    t0 = pl.program_id(2) * Q_BLK
    t_row = t0 + _iota((1, Q_BLK), 1)
    t_row4 = t0 + _iota((1, nq), 1) % Q_BLK
    q_t = q_ref[...]
    gates = jax.nn.sigmoid(gate_ref[...])
    lanes = lambda h: slice(h * Q_BLK, (h + 1) * Q_BLK)

    s_c = _dot(kc_ref[...], q_t)
    cmp_end = _iota((n_cmp, 1), 0) * CMP_STRIDE + (CMP_BLK - 1)
    bias_c = jnp.where(cmp_end <= t_row, 0.0, NEG)
    any_c = (t_row >= CMP_BLK - 1).astype(F32)
    vc_t = vc_ref[...]
    o_c = []
    p_sum = jnp.zeros((n_cmp, Q_BLK), F32)
    for h in range(NSA_HPG):
        e, l = _col_softmax(s_c[:, lanes(h)] + bias_c)
        p = e * (any_c / l)
        p_sum = p_sum + p
        o_c.append(_dot(vc_t, p.astype(BF16)))
    imp = _dot(wimp_ref[...], p_sum, HI)

    blk = _iota((n_blk, Q_BLK), 0)
    cur = t_row // SEL_BLK
    forced = (blk == 0) | (blk == cur) | (blk == cur - 1)
    causal = blk * SEL_BLK <= t_row
    score = jnp.where(causal, jnp.where(forced, FORCE, imp), NEG)
    sel = jnp.zeros((n_blk, Q_BLK), F32)
    for _ in range(n_sel):
        m = jnp.max(score, axis=0, keepdims=True)
        idx = jnp.min(jnp.where(score == m, blk, n_blk), axis=0, keepdims=True)
        hit = blk == idx
        sel = jnp.where(hit, 1.0, sel)
        score = jnp.where(hit, -jnp.inf, score)
    nsel = jnp.where(causal, 1.0 - sel, 1.0).astype(BF16)
    bmat = jnp.concatenate([jnp.concatenate([nsel] * NSA_HPG, axis=1), q_t], axis=0)

    def sel_tile(kt, carry, diagonal):
        m, l, acc = carry
        start = pl.multiple_of(kt * KEY_TILE, KEY_TILE)
        s = _dot(ke_ref[pl.ds(start, KEY_TILE), :], bmat)
        if diagonal:
            s = jnp.where(start + _iota((KEY_TILE, 1), 0) <= t_row4, s, NEG)
        m_new = jnp.maximum(m, jnp.max(s, axis=0, keepdims=True))
        alpha = jnp.exp(m - m_new)
        p = jnp.exp(s - m_new)
        l = alpha * l + jnp.sum(p, axis=0, keepdims=True)
        acc = alpha * acc + _dot(vs_ref[kt], p.astype(BF16))
        return m_new, l, acc

    init = (jnp.full((1, nq), NEG, F32), jnp.zeros((1, nq), F32), jnp.zeros((NSA_DH, nq), F32))
    n_full = t0 // KEY_TILE
    carry = lax.fori_loop(0, n_full, functools.partial(sel_tile, diagonal=False), init)
    _, l_s, acc_s = sel_tile(n_full, carry, True)
    o_s = acc_s / l_s

    wspan = WINDOW + Q_BLK
    wstart = pl.multiple_of(jnp.maximum(t0 - WINDOW, 0), Q_BLK)
    s_w = _dot(kw_ref[pl.ds(wstart, wspan), :], q_t)
    w0 = wstart // Q_BLK
    vwin_t = jnp.concatenate([vw_ref[w0 + i] for i in range(wspan // Q_BLK)], axis=1)
    rel = t_row - (wstart + _iota((wspan, 1), 0))
    bias_w = jnp.where((rel >= 0) & (rel < WINDOW), 0.0, NEG)

    for h in range(NSA_HPG):
        e, l = _col_softmax(s_w[:, lanes(h)] + bias_w)
        o_w = _dot(vwin_t, e.astype(BF16)) / l
        g = gates[:, lanes(h)]
        o = o_c[h] * g[0:1] + o_s[:, lanes(h)] * g[1:2] + o_w * g[2:3]
        o_ref[:, lanes(h)] = o.astype(o_ref.dtype)


def _nsa(q_t, kc, vc_t, ke, vs_t, kw, vw_t, gate, wimp):
    b, _, n_qb, _, nq = q_t.shape
    s = n_qb * Q_BLK
    n_cmp = kc.shape[2]
    n_blk = s // SEL_BLK
    per_g = lambda a: pl.BlockSpec((None, None) + a.shape[2:], lambda i, j, k: (i, j) + (0,) * (a.ndim - 2))
    per_q = lambda rows: pl.BlockSpec((None, None, None, rows, nq), lambda i, j, k: (i, j, k, 0, 0))
    return pl.pallas_call(
        functools.partial(_nsa_body, n_sel=min(N_SEL, n_blk)),
        grid=(b, NSA_GROUPS, n_qb),
        in_specs=[per_q(NSA_DH), per_g(kc), per_g(vc_t), per_g(ke), per_g(vs_t), per_g(kw), per_g(vw_t), per_q(3),
                  pl.BlockSpec(wimp.shape, lambda i, j, k: (0, 0))],
        out_specs=per_q(NSA_DH),
        out_shape=jax.ShapeDtypeStruct((b, NSA_GROUPS, n_qb, NSA_DH, nq), BF16),
        compiler_params=_cparams(("parallel", "parallel", "arbitrary")),
        name="nsa",
    )(q_t, kc, vc_t, ke, vs_t, kw, vw_t, gate, wimp)


def _gdn_prep_body(x_ref, prev_ref, w_ref, qs_ref, q_ref, k_ref, v_ref, pad_ref):
    tm = x_ref.shape[0]
    width = GDN_HEADS * GDN_DH
    first = pl.program_id(1) == 0
    pad_ref[0:8, :] = jnp.where(first, 0.0, prev_ref[...])
    pad_ref[8:8 + tm, :] = x_ref[...]
    conv = pad_ref[8:8 + tm, :] * w_ref[CONV_K - 1:CONV_K, :]
    for j in range(CONV_K - 1):
        sh = CONV_K - 1 - j
        conv = conv + pad_ref[8 - sh:8 - sh + tm, :] * w_ref[j:j + 1, :]
    conv = conv * jax.nn.sigmoid(conv)
    for part, ref in enumerate((q_ref, k_ref, v_ref)):
        for h in range(GDN_HEADS):
            lo = part * width + h * GDN_DH
            c = conv[:, lo:lo + GDN_DH]
            if part < 2:
                c = c * lax.rsqrt(jnp.sum(c * c, axis=-1, keepdims=True) + qs_ref[1:2, 0:1])
            if part == 0:
                c = c * qs_ref[0:1, :]
            ref[:, h * GDN_DH:(h + 1) * GDN_DH] = c


def _gdn_prep(qkv, conv_w, q_scale, tm=256):
    b, s, w3 = qkv.shape
    width = w3 // 3
    out = pl.BlockSpec((None, tm, width), lambda i, j: (i, j, 0))
    return pl.pallas_call(
        _gdn_prep_body,
        grid=(b, s // tm),
        in_specs=[pl.BlockSpec((None, tm, w3), lambda i, j: (i, j, 0)),
                  pl.BlockSpec((None, 8, w3), lambda i, j: (i, jnp.maximum(j * (tm // 8) - 1, 0), 0)),
                  pl.BlockSpec(conv_w.shape, lambda i, j: (0, 0)),
                  pl.BlockSpec(q_scale.shape, lambda i, j: (0, 0))],
        out_specs=[out, out, out],
        out_shape=[jax.ShapeDtypeStruct((b, s, width), F32)] * 3,
        scratch_shapes=[pltpu.VMEM((tm + 8, w3), F32)],
        compiler_params=_cparams(("parallel", "parallel")),
        name="gdn_prep",
    )(qkv, qkv, conv_w, q_scale)


def _gdn_local_body(q_ref, k_ref, kt_ref, v_ref, ac_ref, bc_ref, ar_ref, alog_c_ref, dtb_c_ref, alog_r_ref, dtb_r_ref,
                    u_ref, w_ref, intra_ref, qd_ref, kdt_ref, gl_ref, *, nc):
    c = GDN_CHUNK
    row = _iota((c, c), 0)
    col = _iota((c, c), 1)
    incl = row >= col
    strict = row > col
    tri = incl.astype(F32)
    eye = (row == col).astype(F32)
    xs, ts, rhs_u, rhs_w, where_to = [], [], [], [], []
    for ci in range(nc):
        rs = slice(ci * c, (ci + 1) * c)
        g_c = -jnp.exp(alog_c_ref[...]) * jax.nn.softplus(ac_ref[rs, :] + dtb_c_ref[...])
        g_r = -jnp.exp(alog_r_ref[...]) * jax.nn.softplus(ar_ref[:, rs] + dtb_r_ref[...])
        gc_all = _dot(tri, g_c, HI)
        gr_all = _dot_nt(g_r, tri, HI)
        beta_all = jax.nn.sigmoid(bc_ref[rs, :])
        gl_ref[ci] = jnp.exp(gc_all[c - 1:c, :])
        for h in range(GDN_HEADS):
            hs = slice(h * GDN_DH, (h + 1) * GDN_DH)
            gc = gc_all[:, h:h + 1]
            gr = gr_all[h:h + 1, :]
            g_last = gc_all[c - 1:c, h:h + 1]
            beta = beta_all[:, h:h + 1]
            q = q_ref[rs, hs]
            kt = kt_ref[h, ci]
            kb = k_ref[rs, hs] * beta
            decay = jnp.where(incl, jnp.exp(jnp.where(incl, gc - gr, 0.0)), 0.0)
            a = jnp.where(strict, _dot(kb, kt) * decay, 0.0)
            eg = jnp.exp(gc)
            xs.append(_split(-a))
            ts.append(eye - a)
            rhs_u.append(v_ref[rs, hs] * beta)
            rhs_w.append(kb * eg)
            where_to.append((rs, hs))
            intra_ref[h, ci] = jnp.where(incl, _dot(q, kt) * decay, 0.0)
            qd_ref[rs, hs] = q * eg
            kdt_ref[h, ci] = kt * jnp.exp(g_last - gr)
    for _ in range(int(math.log2(c)) - 1):
        xs = [_dot3(x, x) for x in xs]
        xs = [_split(x) for x in xs]
        ts = [t + _dot3(_split(t), x) for t, x in zip(ts, xs)]
    for t, ru, rw, (rs, hs) in zip(ts, rhs_u, rhs_w, where_to):
        t = _split(t)
        u_ref[rs, hs] = _dot3(t, _split(ru))
        w_ref[rs, hs] = _dot3(t, _split(rw))


def _gdn_local(q, k, kt5, v, a_col, b_col, a_row, alog, dtb, nc=2):
    b, s, width = q.shape
    n = s // GDN_CHUNK
    hh = GDN_HEADS
    rows = nc * GDN_CHUNK
    tok = pl.BlockSpec((None, rows, width), lambda i, j: (i, j, 0))
    per5 = lambda d0, d1: pl.BlockSpec((None, hh, nc, d0, d1), lambda i, j: (i, 0, j, 0, 0))
    small = lambda shape: pl.BlockSpec(shape, lambda i, j: (0,) * len(shape))
    return pl.pallas_call(
        functools.partial(_gdn_local_body, nc=nc),
        grid=(b, n // nc),
        in_specs=[tok, tok, per5(GDN_DH, GDN_CHUNK), tok,
                  pl.BlockSpec((None, rows, hh), lambda i, j: (i, j, 0)),
                  pl.BlockSpec((None, rows, hh), lambda i, j: (i, j, 0)),
                  pl.BlockSpec((None, hh, rows), lambda i, j: (i, 0, j)),
                  small((1, hh)), small((1, hh)), small((hh, 1)), small((hh, 1))],
        out_specs=[tok, tok, per5(GDN_CHUNK, GDN_CHUNK), tok, per5(GDN_DH, GDN_CHUNK),
                   pl.BlockSpec((None, nc, 1, hh), lambda i, j: (i, j, 0, 0))],
        out_shape=[jax.ShapeDtypeStruct((b, s, width), F32), jax.ShapeDtypeStruct((b, s, width), F32),
                   jax.ShapeDtypeStruct((b, hh, n, GDN_CHUNK, GDN_CHUNK), F32),
                   jax.ShapeDtypeStruct((b, s, width), F32),
                   jax.ShapeDtypeStruct((b, hh, n, GDN_DH, GDN_CHUNK), F32),
                   jax.ShapeDtypeStruct((b, n, 1, hh), F32)],
        compiler_params=_cparams(("parallel", "parallel")),
        name="gdn_local",
    )(q, k, kt5, v, a_col, b_col, a_row, alog.reshape(1, hh), dtb.reshape(1, hh), alog.reshape(hh, 1),
      dtb.reshape(hh, 1))


def _gdn_scan_body(u_ref, w_ref, intra_ref, qd_ref, kdt_ref, gl_ref, z_ref, gain_ref, o_ref, state_ref, *, nc):
    c = GDN_CHUNK

    @pl.when(pl.program_id(1) == 0)
    def _():
        state_ref[...] = jnp.zeros_like(state_ref)

    heads = range(GDN_HEADS)
    hs = [slice(h * GDN_DH, (h + 1) * GDN_DH) for h in heads]
    for ci in range(nc):
        rs = slice(ci * c, (ci + 1) * c)
        gl = gl_ref[ci]
        st = [state_ref[h] for h in heads]
        st2 = [_split(s) for s in st]
        v_new = [u_ref[rs, hs[h]] - _dot3(_split(w_ref[rs, hs[h]]), st2[h]) for h in heads]
        o_st = [_dot3(_split(qd_ref[rs, hs[h]]), st2[h]) for h in heads]
        vn2 = [_split(v) for v in v_new]
        for h in heads:
            state_ref[h] = st[h] * gl[:, h:h + 1] + _dot3(_split(kdt_ref[h, ci]), vn2[h])
        for h in heads:
            o = o_st[h] + _dot3(_split(intra_ref[h, ci]), vn2[h])
            y = o * lax.rsqrt(jnp.mean(o * o, axis=-1, keepdims=True) + gain_ref[1:2, 0:1]) * gain_ref[0:1, :]
            z = z_ref[rs, hs[h]]
            o_ref[rs, hs[h]] = (y * (z * jax.nn.sigmoid(z))).astype(o_ref.dtype)


def _gdn_scan(u, w, intra, qd, kdt, gl, z, gain, nc=2):
    b, s, width = u.shape
    n = s // GDN_CHUNK
    hh = GDN_HEADS
    rows = nc * GDN_CHUNK
    tok = pl.BlockSpec((None, rows, width), lambda i, j: (i, j, 0))
    per5 = lambda d0, d1: pl.BlockSpec((None, hh, nc, d0, d1), lambda i, j: (i, 0, j, 0, 0))
    return pl.pallas_call(
        functools.partial(_gdn_scan_body, nc=nc),
        grid=(b, n // nc),
        in_specs=[tok, tok, per5(GDN_CHUNK, GDN_CHUNK), tok, per5(GDN_DH, GDN_CHUNK),
                  pl.BlockSpec((None, nc, 1, hh), lambda i, j: (i, j, 0, 0)), tok,
                  pl.BlockSpec(gain.shape, lambda i, j: (0, 0))],
        out_specs=tok,
        out_shape=jax.ShapeDtypeStruct((b, s, width), BF16),
        scratch_shapes=[pltpu.VMEM((hh, GDN_DH, GDN_DH), F32)],
        compiler_params=_cparams(("parallel", "arbitrary")),
        name="gdn_scan",
    )(u, w, intra, qd, kdt, gl, z, gain)


def _outproj_body(oa_ref, ob_ref, x_ref, wa_ref, wb_ref, g_ref, x2_ref, h2_ref):
    x2 = x_ref[...] + _dot(oa_ref[...], wa_ref[...]) + _dot(ob_ref[...], wb_ref[...])
    x2_ref[...] = x2
    r = lax.rsqrt(jnp.mean(x2 * x2, axis=-1, keepdims=True) + EPS)
    h2_ref[...] = (x2 * r * g_ref[...]).T.astype(BF16)


def _outproj(oa, ob, x2d, wa, wb, g, tm=512):
    t = x2d.shape[0]
    row = lambda n: pl.BlockSpec((tm, n), lambda i: (i, 0))
    full = lambda shape: pl.BlockSpec(shape, lambda i: (0,) * len(shape))
    return pl.pallas_call(
        _outproj_body,
        grid=(t // tm,),
        in_specs=[row(oa.shape[1]), row(ob.shape[1]), row(D_MODEL), full(wa.shape), full(wb.shape), full((1, D_MODEL))],
        out_specs=[row(D_MODEL), pl.BlockSpec((D_MODEL, tm), lambda i: (0, i))],
        out_shape=[jax.ShapeDtypeStruct((t, D_MODEL), F32), jax.ShapeDtypeStruct((D_MODEL, t), BF16)],
        compiler_params=_cparams(("parallel",)),
        name="outproj",
    )(oa, ob, x2d, wa, wb, g)


def _top_ranks(s, k, want_rank):
    n = s.shape[0]
    rows = _iota(s.shape, 0)
    rank = jnp.full(s.shape, float(k), F32) if want_rank else None
    vals, idxs = [], []
    for r in range(k):
        m = jnp.max(s, axis=0, keepdims=True)
        idx = jnp.min(jnp.where(s == m, rows, n), axis=0, keepdims=True)
        hit = rows == idx
        if want_rank:
            rank = jnp.where(hit, float(r), rank)
        s = jnp.where(hit, -jnp.inf, s)
        vals.append(m)
        idxs.append(idx)
    return rank, vals, idxs


def _route_body(h_ref, wq_ref, sk_ref, r2_ref, e2_ref, lam_ref, w1_ref):
    tm = h_ref.shape[1]
    kk = PEER_TOPK
    half = PEER_DKEY // 2
    qt = _dot(wq_ref[...], h_ref[...])
    ridx = _iota((80, tm), 0)
    grp = ridx // 8
    sub = ridx % 8
    pos = jnp.where(grp <= 1, ridx, jnp.where(grp <= 8, (grp - 1) * kk + sub, (8 + sub) * kk))
    for h in range(PEER_HEADS):
        s1 = _dot(sk_ref[0], qt[h * PEER_DKEY:h * PEER_DKEY + half], HI)
        s2 = _dot(sk_ref[1], qt[h * PEER_DKEY + half:(h + 1) * PEER_DKEY], HI)
        _, v1, idx1 = _top_ranks(s1, kk, False)
        rank2, v2, _ = _top_ranks(s2, kk, True)
        v2lo = jnp.concatenate(v2[0:8], axis=0)
        v2hi = jnp.concatenate(v2[8:16], axis=0)
        v1hi = jnp.concatenate(v1[8:16], axis=0)
        cand = jnp.concatenate([v1[0] + v2lo, v1[0] + v2hi] + [v1[r] + v2lo for r in range(1, 8)] + [v1hi + v2[0]],
                               axis=0)
        cmax = v1[0] + v2[0]
        z = jnp.zeros((1, tm), F32)
        chosen = jnp.zeros((80, tm), F32)
        for _ in range(kk):
            m = jnp.max(cand, axis=0, keepdims=True)
            p = jnp.min(jnp.where(cand == m, pos, 4 * kk * kk), axis=0, keepdims=True)
            hit = pos == p
            chosen = jnp.where(hit, 1.0, chosen)
            cand = jnp.where(hit, -jnp.inf, cand)
            z = z + jnp.exp(m - cmax)
        cnt = [jnp.sum(chosen[0:16], axis=0, keepdims=True)]
        cnt += [jnp.sum(chosen[8 * (r + 1):8 * (r + 2)], axis=0, keepdims=True) for r in range(1, 8)]
        cnt += [chosen[72 + r:73 + r] for r in range(8)]
        lam = jnp.zeros((N_KEYS, tm), F32)
        keys = _iota((N_KEYS, tm), 0)
        for r in range(kk):
            lam = jnp.where(keys == idx1[r], cnt[r], lam)
        r2_ref[h] = rank2.astype(BF16)
        e2_ref[h] = jnp.exp(s2 - v2[0]).astype(BF16)
        lam_ref[h] = lam
        w1_ref[h] = jnp.exp(s1 - v1[0]) / z


def _route(h2_t, wq_t, subkeys, tm=256):
    t = h2_t.shape[1]
    full = lambda shape: pl.BlockSpec(shape, lambda i: (0,) * len(shape))
    out = pl.BlockSpec((PEER_HEADS, N_KEYS, tm), lambda i: (0, 0, i))
    return pl.pallas_call(
        _route_body,
        grid=(t // tm,),
        in_specs=[pl.BlockSpec((D_MODEL, tm), lambda i: (0, i)), full(wq_t.shape), full(subkeys.shape)],
        out_specs=[out] * 4,
        out_shape=[jax.ShapeDtypeStruct((PEER_HEADS, N_KEYS, t), dt) for dt in (BF16, BF16, F32, F32)],
        compiler_params=_cparams(("parallel",)),
        name="route",
    )(h2_t, wq_t, subkeys)


def _peer_body(h_ref, u_ref, vt_ref, r2_ref, e2_ref, lam_ref, w1_ref, x2_ref, o_ref, acc_ref, coef_ref, *, sub,
               esub):
    te = u_ref.shape[0]
    tm = h_ref.shape[1]
    e = pl.program_id(1)

    @pl.when(e == 0)
    def _():
        acc_ref[...] = jnp.zeros_like(acc_ref)

    n1 = te // N_KEYS
    i1_lo = pl.multiple_of(e * n1, n1)
    hq = h_ref[...]
    act = jax.nn.gelu(_dot(u_ref[...], hq))
    per = esub // N_KEYS
    for s in range(te // esub):
        es = slice(s * esub, (s + 1) * esub)
        for kk in range(per):
            k = s * per + kk
            ks = slice(k * N_KEYS, (k + 1) * N_KEYS)
            for c in range(tm // sub):
                cs = slice(c * sub, (c + 1) * sub)
                g = jnp.zeros((N_KEYS, sub), BF16)
                for h in range(PEER_HEADS):
                    lam = jnp.broadcast_to(lam_ref[h, pl.ds(i1_lo, n1), cs][k:k + 1].astype(BF16), (N_KEYS, sub))
                    w1 = jnp.broadcast_to(w1_ref[h, pl.ds(i1_lo, n1), cs][k:k + 1].astype(BF16), (N_KEYS, sub))
                    g = g + jnp.where(r2_ref[h, :, cs] < lam, e2_ref[h, :, cs], jnp.zeros_like(w1)) * w1
                coef_ref[ks, cs] = act[ks, cs].astype(BF16) * g
        acc_ref[...] += _dot(vt_ref[:, es], coef_ref[es, :])

    @pl.when(e == pl.num_programs(1) - 1)
    def _():
        o_ref[...] = x2_ref[...] + acc_ref[...].T


def _peer(h2_t, u, vt, r2, e2, lam, w1, x2, tm=512, te=2048, sub=256, esub=256):
    t = h2_t.shape[1]
    n_exp = u.shape[0]
    rt = pl.BlockSpec((PEER_HEADS, N_KEYS, tm), lambda i, j: (0, 0, i))
    return pl.pallas_call(
        functools.partial(_peer_body, sub=sub, esub=esub),
        grid=(t // tm, n_exp // te),
        in_specs=[pl.BlockSpec((D_MODEL, tm), lambda i, j: (0, i)),
                  pl.BlockSpec((te, D_MODEL), lambda i, j: (j, 0)),
                  pl.BlockSpec((D_MODEL, te), lambda i, j: (0, j)),
                  rt, rt, rt, rt,
                  pl.BlockSpec((tm, D_MODEL), lambda i, j: (i, 0))],
        out_specs=pl.BlockSpec((tm, D_MODEL), lambda i, j: (i, 0)),
        out_shape=jax.ShapeDtypeStruct((t, D_MODEL), F32),
        scratch_shapes=[pltpu.VMEM((D_MODEL, tm), F32), pltpu.VMEM((te, tm), BF16)],
        compiler_params=_cparams(("parallel", "arbitrary")),
        name="peer",
    )(h2_t, u, vt, r2, e2, lam, w1, x2)


def _rope_tables(pos):
    half = NSA_DH // 2
    inv = jnp.exp(_real_const(-math.log(ROPE_THETA) * 2.0) * jnp.arange(half, dtype=F32) / NSA_DH)
    ang = pos.astype(F32)[:, None] * inv[None, :]
    cos, sin = jnp.cos(ang), jnp.sin(ang)
    return jnp.concatenate([cos, cos], axis=-1), jnp.concatenate([-sin, sin], axis=-1)


def _swap_halves(a):
    half = a.shape[-1] // 2
    return jnp.concatenate([a[..., half:], a[..., :half]], axis=-1)


def _importance_matrix(n_cmp_pad, n_blk):
    r = SEL_BLK // CMP_STRIDE
    l = CMP_BLK // CMP_STRIDE
    c = jnp.arange(n_cmp_pad)[:, None]
    j = jnp.arange(n_blk)[None, :]
    o = c - r * j
    wgt = jnp.minimum(o + l, r) - jnp.maximum(o, 0)
    return jnp.where((o >= -(l - 1)) & (o < r), wgt, 0).astype(F32)


def _layer(x, attn_norm, w_in, q_gain, k_gain, cmp_pos, cmp_w1, cmp_w2, conv_w, a_log, dt_bias, out_gain,
           w_out, ffn_norm, w_query, subkeys, u_tab, v_tab):
    b, s, d = x.shape
    t = b * s
    x2d = x.reshape(t, d)
    nw = NSA_HEADS * NSA_DH
    nkv = NSA_GROUPS * NSA_DH
    gw = GDN_HEADS * GDN_DH

    o_q, o_kv, o_gate = 0, nw, nw + 6 * nkv
    o_qkvb = o_gate + 3 * NSA_HEADS
    o_z = o_qkvb + 3 * gw
    o_a = o_z + gw
    o_b = o_a + GDN_HEADS
    n_small = 3 * NSA_HEADS + 2 * GDN_HEADS
    w_perm = jnp.concatenate([w_in[:, o_q:o_gate], w_in[:, o_qkvb:o_a], w_in[:, o_gate:o_qkvb], w_in[:, o_a:],
                              jnp.zeros((d, LANES - n_small), w_in.dtype)], axis=1).astype(BF16)
    seg = jnp.kron(jnp.eye(LANES // NSA_DH, dtype=F32), jnp.full((NSA_DH, NSA_DH), 1.0 / NSA_DH, F32)).astype(BF16)
    cos, sin = _rope_tables(jnp.arange(s))
    cos2, sin2 = jnp.tile(cos, (1, 2)), jnp.tile(sin, (1, 2))
    qg = jnp.tile(q_gain, 2).reshape(1, LANES)
    kg = jnp.tile(k_gain, (1, 2))
    q_a, kv_a, qkv_b, z_b, small = _proj(x2d, attn_norm.reshape(1, d), w_perm, seg, cos2, sin2, qg, kg, s)

    def heads(a):
        return a.reshape(b, s, NSA_GROUPS, NSA_DH).transpose(0, 2, 1, 3)

    kv = [heads(kv_a[:, i * nkv:(i + 1) * nkv]) for i in range(6)]
    n16 = s // CMP_STRIDE
    a16k = kv[0].reshape(b, NSA_GROUPS, n16, CMP_STRIDE * NSA_DH)
    a16v = kv[1].reshape(b, NSA_GROUPS, n16, CMP_STRIDE * NSA_DH)
    cmp_end = jnp.arange(n16) * CMP_STRIDE + CMP_BLK - 1
    ccos, csin = _rope_tables(cmp_end)
    pos8 = jnp.broadcast_to(cmp_pos.reshape(2, 1, CMP_BLK * NSA_DH), (2, 8, CMP_BLK * NSA_DH)).astype(BF16)
    gelu_consts = jnp.stack([_real_const(math.sqrt(2.0 / math.pi)), _real_const(0.044715)]).reshape(1, 2)
    kc, vc = _compress(a16k, a16v, cmp_w1.astype(BF16), cmp_w2.astype(BF16), _swap_halves(cmp_w2[0]).astype(BF16),
                       pos8, k_gain[0:1], _swap_halves(k_gain[0:1]), ccos, csin, gelu_consts)
    n_qb = s // Q_BLK
    n_blk = s // SEL_BLK
    nq = NSA_HPG * Q_BLK
    q_t = q_a.reshape(b, n_qb, Q_BLK, NSA_GROUPS, NSA_HPG, NSA_DH).transpose(0, 3, 1, 5, 4, 2)
    q_t = q_t.reshape(b, NSA_GROUPS, n_qb, NSA_DH, nq)

    def tiles_t(a, tile):
        return a.astype(BF16).reshape(b, NSA_GROUPS, s // tile, tile, NSA_DH).transpose(0, 1, 2, 4, 3)
    gate = small[:, 0:3 * NSA_HEADS].reshape(b, n_qb, Q_BLK, NSA_GROUPS, NSA_HPG, 3)
    gate = gate.transpose(0, 3, 1, 5, 4, 2).reshape(b, NSA_GROUPS, n_qb, 3, nq)
    wimp_t = _importance_matrix(n16, n_blk).T
    own_blk = jnp.arange(s)[:, None] // SEL_BLK == jnp.arange(n_blk)[None, :]
    off = jnp.broadcast_to(jnp.where(own_blk, -MASK_OFF, 0.0).astype(BF16), (b, NSA_GROUPS, s, n_blk))
    ke = jnp.concatenate([off, kv[2].astype(BF16)], axis=-1)
    o_t = _nsa(q_t, kc, vc.transpose(0, 1, 3, 2), ke, tiles_t(kv[3], KEY_TILE), kv[4].astype(BF16),
               tiles_t(kv[5], Q_BLK), gate, wimp_t)
    o_nsa = o_t.reshape(b, NSA_GROUPS, n_qb, NSA_DH, NSA_HPG, Q_BLK).transpose(0, 2, 5, 1, 4, 3).reshape(t, nw)

    n_chunk = s // GDN_CHUNK
    eps_row = jnp.broadcast_to(_real_const(EPS), (1, GDN_DH))
    q_scale = jnp.concatenate([jnp.broadcast_to(_real_const(GDN_DH ** -0.5), (1, GDN_DH)), eps_row], axis=0)
    gq, gk, gv = _gdn_prep(qkv_b.reshape(b, s, 3 * gw), conv_w, q_scale)
    kt5 = gk.reshape(b, n_chunk, GDN_CHUNK, GDN_HEADS, GDN_DH).transpose(0, 3, 1, 4, 2)
    a_col = small[:, 3 * NSA_HEADS:3 * NSA_HEADS + GDN_HEADS].reshape(b, s, GDN_HEADS)
    b_col = small[:, 3 * NSA_HEADS + GDN_HEADS:n_small].reshape(b, s, GDN_HEADS)
    u, w, intra, qd, kdt, gl = _gdn_local(gq, gk, kt5, gv, a_col, b_col, a_col.transpose(0, 2, 1), a_log, dt_bias)
    gain_eps = jnp.concatenate([out_gain.reshape(1, GDN_DH), eps_row], axis=0)
    o_gdn = _gdn_scan(u, w, intra, qd, kdt, gl, z_b.reshape(b, s, gw), gain_eps).reshape(t, gw)

    w_out_b = w_out.astype(BF16)
    x2, h2_t = _outproj(o_nsa, o_gdn, x2d, w_out_b[:nw], w_out_b[nw:], ffn_norm.reshape(1, d))

    r2, e2, lam, w1 = _route(h2_t, w_query.T.astype(BF16), subkeys)
    out = _peer(h2_t, u_tab.astype(BF16), v_tab.T.astype(BF16), r2, e2, lam, w1, x2)
    return out.reshape(b, s, d)


def kernel(x, attn_norm, w_in, nsa_q_gain, nsa_k_gain, cmp_pos, cmp_w1, cmp_w2, gdn_conv, gdn_a_log, gdn_dt_bias,
           gdn_out_gain, w_out, ffn_norm, peer_w_query, peer_subkeys, peer_u, peer_v):
    for l in range(attn_norm.shape[0]):
        x = _layer(x, attn_norm[l], w_in[l], nsa_q_gain[l], nsa_k_gain[l], cmp_pos[l], cmp_w1[l], cmp_w2[l],
                   gdn_conv[l], gdn_a_log[l], gdn_dt_bias[l], gdn_out_gain[l], w_out[l], ffn_norm[l],
                   peer_w_query[l], peer_subkeys[l], peer_u[l], peer_v[l])
    return x
```
